```python
import math
import jax, jax.numpy as jnp
from jax import lax
import numpy as np

D_MODEL = 2048
BATCH = 8
SEQ = 2048
DEPTH = 1
DEC_BATCH = 32
DEC_SEQ = 1
PAST_LEN = 16384
PAGE_SIZE = 128

N_META = 16
H_A = 4
DH_A = 128
W_A = H_A * 2 * DH_A
H_R = 8
DK_R = 128
DV_R = 128
W_R = H_R * DV_R
D_FF = 4 * D_MODEL
CHUNK = 128
Q_BLOCK = 128
ROPE_BASE = 10000.0
EPS = 1e-6

kernel_name = "hybrid_diffattn_retention_decode_step"


def rmsnorm(x, g):
    xf = x.astype(jnp.float32)
    y = xf * lax.rsqrt(jnp.mean(xf * xf, axis=-1, keepdims=True) + EPS)
    return (y * g.astype(jnp.float32)).astype(x.dtype)


def in_project(a, w_in):
    z = jnp.einsum('btd,de->bte', a, w_in)
    sizes = [W_A, W_A, W_A, H_R * DK_R, H_R * DK_R, W_R, W_R, D_MODEL, D_MODEL]
    offsets = [int(o) for o in np.cumsum(sizes)[:-1]]
    q_a, k_a, v_a, q_r, k_r, v_r, g_r, g_a, g_b = jnp.split(z, offsets, axis=-1)
    b, t = a.shape[0], a.shape[1]
    return (q_a.reshape(b, t, H_A, 2 * DH_A), k_a.reshape(b, t, H_A, 2 * DH_A),
            v_a.reshape(b, t, H_A, 2 * DH_A), q_r.reshape(b, t, H_R, DK_R),
            k_r.reshape(b, t, H_R, DK_R), v_r.reshape(b, t, H_R, DV_R), g_r, g_a, g_b)


def rotary(x, pos):
    half = x.shape[-1] // 2
    inv = 1.0 / (ROPE_BASE ** jnp.linspace(0.0, 1.0, half, dtype=jnp.float32))
    ang = pos.astype(jnp.float32)[:, None] * inv[None, :]
    cos = jnp.cos(ang)[None, :, None, :]
    sin = jnp.sin(ang)[None, :, None, :]
    xf = x.astype(jnp.float32)
    x1, x2 = xf[..., :half], xf[..., half:]
    return jnp.concatenate([x1 * cos - x2 * sin, x1 * sin + x2 * cos], axis=-1)


def diff_lambda(lq1, lk1, lq2, lk2, lam_init):
    return (jnp.exp(jnp.sum(lq1.astype(jnp.float32) * lk1.astype(jnp.float32)))
            - jnp.exp(jnp.sum(lq2.astype(jnp.float32) * lk2.astype(jnp.float32))) + lam_init)


def prompt_diff_attention(q, k, v, lam):
    b, l = q.shape[0], q.shape[1]
    nb = -(-l // Q_BLOCK)
    pad = nb * Q_BLOCK - l
    qf = q.astype(jnp.float32) * (DH_A ** -0.5)
    kf = k.astype(jnp.float32)
    k1, k2 = kf[..., :DH_A], kf[..., DH_A:]
    vf = v.astype(jnp.float32)
    qb = jnp.pad(qf, ((0, 0), (0, pad), (0, 0), (0, 0))).reshape(b, nb, Q_BLOCK, H_A, 2 * DH_A).swapaxes(0, 1)
    kpos = jnp.arange(l)

    def block(args):
        qblk, start = args
        qpos = start + jnp.arange(Q_BLOCK)
        mask = (kpos[None, :] <= qpos[:, None])[None, None]
        s1 = jnp.einsum('bqhd,bkhd->bhqk', qblk[..., :DH_A], k1)
        s2 = jnp.einsum('bqhd,bkhd->bhqk', qblk[..., DH_A:], k2)
        p = (jax.nn.softmax(jnp.where(mask, s1, -jnp.inf), axis=-1)
             - lam * jax.nn.softmax(jnp.where(mask, s2, -jnp.inf), axis=-1))
        return jnp.einsum('bhqk,bkhd->bqhd', p, vf)

    o = lax.map(block, (qb, jnp.arange(nb) * Q_BLOCK))
    return o.swapaxes(0, 1).reshape(b, nb * Q_BLOCK, H_A, 2 * DH_A)[:, :l]


def sample_diff_attention(q, k_new, v_new, k_past, v_past, lam):
    t = q.shape[1]
    p_len = k_past.shape[1]
    qf = q.astype(jnp.float32) * (DH_A ** -0.5)
    kp = k_past.astype(jnp.float32)
    kn = k_new.astype(jnp.float32)
    causal = jnp.tril(jnp.ones((t, t), dtype=bool))[None, None]

    def probs(qm, kpm, knm):
        s_past = jnp.einsum('bqhd,bkhd->bhqk', qm, kpm)
        s_new = jnp.where(causal, jnp.einsum('bqhd,bkhd->bhqk', qm, knm), -jnp.inf)
        return jax.nn.softmax(jnp.concatenate([s_past, s_new], axis=-1), axis=-1)

    p = (probs(qf[..., :DH_A], kp[..., :DH_A], kn[..., :DH_A])
         - lam * probs(qf[..., DH_A:], kp[..., DH_A:], kn[..., DH_A:]))
    return (jnp.einsum('bhqk,bkhd->bqhd', p[..., :p_len], v_past.astype(jnp.float32))
            + jnp.einsum('bhqk,bkhd->bqhd', p[..., p_len:], v_new.astype(jnp.float32)))


def attn_subnorm(o, gain, lam_init):
    b, t = o.shape[0], o.shape[1]
    y = o * lax.rsqrt(jnp.mean(o * o, axis=-1, keepdims=True) + EPS)
    y = y * gain.astype(jnp.float32).reshape(H_A, 2 * DH_A) * (1.0 - lam_init)
    return y.reshape(b, t, W_A)


def retention_chunk(q, k, v, s, log_gamma):
    c = q.shape[1]
    idx = jnp.arange(c, dtype=jnp.float32)
    lg = log_gamma
    diff = idx[:, None] - idx[None, :]
    decay = jnp.where(diff[None] >= 0, jnp.exp(jnp.maximum(diff, 0.0)[None] * lg[:, None, None]), 0.0)
    inner = jnp.einsum('bqhd,bkhd->bhqk', q, k) * decay[None]
    decay_q = jnp.exp((idx + 1.0)[:, None] * lg[None, :])
    decay_k = jnp.exp((c - 1.0 - idx)[:, None] * lg[None, :])
    o = (jnp.einsum('bhqk,bkhv->bqhv', inner, v)
         + jnp.einsum('bqhd,bhdv->bqhv', q, s) * decay_q[None, :, :, None])
    s_new = (jnp.exp(c * lg)[None, :, None, None] * s
             + jnp.einsum('bkhd,bkhv->bhdv', k * decay_k[None, :, :, None], v))
    return o, s_new


def prompt_retention(q, k, v, log_gamma):
    b, l = q.shape[0], q.shape[1]
    pad = CHUNK - N_META
    nc = (l + pad) // CHUNK

    def to_chunks(t):
        tp = jnp.pad(t, ((0, 0), (pad, 0), (0, 0), (0, 0)))
        return tp.reshape(b, nc, CHUNK, t.shape[2], t.shape[3]).swapaxes(0, 1)

    s0 = jnp.zeros((b, H_R, DK_R, DV_R), jnp.float32)

    def step(s, qkv):
        o, s = retention_chunk(qkv[0], qkv[1], qkv[2], s, log_gamma)
        return s, o

    s_fin, o = lax.scan(step, s0, (to_chunks(q), to_chunks(k), to_chunks(v)))
    o = o.swapaxes(0, 1).reshape(b, nc * CHUNK, H_R, DV_R)[:, pad:]
    return o, s_fin


def ret_subnorm(o, gain, g_r):
    b, t = o.shape[0], o.shape[1]
    mu = jnp.mean(o, axis=-1, keepdims=True)
    var = jnp.mean((o - mu) ** 2, axis=-1, keepdims=True)
    y = ((o - mu) * lax.rsqrt(var + EPS)).reshape(b, t, W_R) * gain.astype(jnp.float32)
    return jax.nn.silu(g_r.astype(jnp.float32)) * y


def block_output(h, y_a, y_r, g_a, g_b, w_branch, w_out, norm_mlp, w_ff1, w_ff2):
    dt = h.dtype
    br_a = jnp.einsum('btw,wd->btd', y_a.astype(dt), w_branch[0])
    br_b = jnp.einsum('btw,wd->btd', y_r.astype(dt), w_branch[1])
    merged = jax.nn.sigmoid(g_a) * br_a + jax.nn.sigmoid(g_b) * br_b
    h = h + jnp.einsum('btd,de->bte', merged, w_out)
    m = rmsnorm(h, norm_mlp)
    u = jax.nn.relu(jnp.einsum('btd,df->btf', m, w_ff1))
    return h + jnp.einsum('btf,fd->btd', u * u, w_ff2)


def setup_inputs(seed: int = 0) -> dict:
    key = jax.random.key(seed)
    ks = jax.random.split(key, 24)
    f32 = jnp.float32
    n_pages = PAST_LEN // PAGE_SIZE
    n_used = DEC_BATCH * n_pages
    n_pool = n_used + max(1, n_used // 4)
    n_in = 3 * W_A + 2 * H_R * DK_R + 2 * W_R + 2 * D_MODEL

    def nrm(k, shape, scale):
        return jax.random.normal(k, shape, f32) * scale

    def gain(k, shape):
        return 1.0 + 0.02 * jax.random.normal(k, shape, f32)

    page_table = jax.random.permutation(ks[0], n_pool)[:n_used].reshape(DEC_BATCH, n_pages).astype(jnp.int32)
    return {
        'x_prompt': nrm(ks[1], (BATCH, SEQ, D_MODEL), 1.0),
        'x_sample': nrm(ks[2], (DEC_BATCH, DEC_SEQ, D_MODEL), 1.0),
        'cache_k': nrm(ks[3], (DEPTH, n_pool, PAGE_SIZE, H_A, 2 * DH_A), 1.0),
        'cache_v': nrm(ks[4], (DEPTH, n_pool, PAGE_SIZE, H_A, 2 * DH_A), 1.0),
        'state_ret': nrm(ks[5], (DEPTH, DEC_BATCH, H_R, DK_R, DV_R), 1.0),
        'page_table': page_table,
        'meta_tokens': nrm(ks[6], (N_META, D_MODEL), 1.0),
        'norm_mix': gain(ks[7], (DEPTH, D_MODEL)),
        'w_in': nrm(ks[8], (DEPTH, D_MODEL, n_in), D_MODEL ** -0.5),
        'lambda_q1': nrm(ks[9], (DEPTH, DH_A), 0.1),
        'lambda_k1': nrm(ks[10], (DEPTH, DH_A), 0.1),
        'lambda_q2': nrm(ks[11], (DEPTH, DH_A), 0.1),
        'lambda_k2': nrm(ks[12], (DEPTH, DH_A), 0.1),
        'attn_subln': gain(ks[13], (DEPTH, W_A)),
        'ret_subln': gain(ks[14], (DEPTH, W_R)),
        'w_branch': nrm(ks[15], (DEPTH, 2, W_A, D_MODEL), W_A ** -0.5),
        'w_out': nrm(ks[16], (DEPTH, D_MODEL, D_MODEL), D_MODEL ** -0.5),
        'norm_mlp': gain(ks[17], (DEPTH, D_MODEL)),
        'w_ff1': nrm(ks[18], (DEPTH, D_MODEL, D_FF), D_MODEL ** -0.5),
        'w_ff2': nrm(ks[19], (DEPTH, D_FF, D_MODEL), D_FF ** -0.5),
        'norm_final': gain(ks[20], (D_MODEL,)),
    }


def reference(x_prompt, x_sample, cache_k, cache_v, state_ret, page_table, meta_tokens,
              norm_mix, w_in, lambda_q1, lambda_k1, lambda_q2, lambda_k2, attn_subln, ret_subln,
              w_branch, w_out, norm_mlp, w_ff1, w_ff2, norm_final):
    b = x_prompt.shape[0]
    dec_b, t_s = x_sample.shape[0], x_sample.shape[1]
    meta = jnp.broadcast_to(meta_tokens.astype(x_prompt.dtype)[None], (b, N_META, D_MODEL))
    h_p = jnp.concatenate([meta, x_prompt], axis=1)
    h_s = x_sample
    l_p = h_p.shape[1]
    pos_p = jnp.arange(l_p, dtype=jnp.int32)
    pos_s = PAST_LEN + jnp.arange(t_s, dtype=jnp.int32)
    log_gamma = jnp.log1p(-(2.0 ** (-5.0 - jnp.arange(H_R, dtype=jnp.float32))))
    k_ps, v_ps, s_ps, k_ss, v_ss, s_ss = [], [], [], [], [], []
    for l in range(DEPTH):
        lam_init = 0.8 - 0.6 * math.exp(-0.3 * l)
        lam = diff_lambda(lambda_q1[l], lambda_k1[l], lambda_q2[l], lambda_k2[l], lam_init)

        a = rmsnorm(h_p, norm_mix[l])
        q_a, k_a, v_a, q_r, k_r, v_r, g_r, g_a, g_b = in_project(a, w_in[l])
        y_a = attn_subnorm(prompt_diff_attention(q_a, k_a, v_a, lam), attn_subln[l], lam_init)
        qr = rotary(q_r, pos_p)
        kr = rotary(k_r, pos_p) * (DK_R ** -0.5)
        o_r, s_p = prompt_retention(qr, kr, v_r.astype(jnp.float32), log_gamma)
        y_r = ret_subnorm(o_r, ret_subln[l], g_r)
        h_p = block_output(h_p, y_a, y_r, g_a, g_b, w_branch[l], w_out[l], norm_mlp[l], w_ff1[l], w_ff2[l])
        k_ps.append(k_a)
        v_ps.append(v_a)
        s_ps.append(s_p)

        a = rmsnorm(h_s, norm_mix[l])
        q_a, k_a, v_a, q_r, k_r, v_r, g_r, g_a, g_b = in_project(a, w_in[l])
        k_past = cache_k[l, page_table].reshape(dec_b, -1, H_A, 2 * DH_A)
        v_past = cache_v[l, page_table].reshape(dec_b, -1, H_A, 2 * DH_A)
        y_a = attn_subnorm(sample_diff_attention(q_a, k_a, v_a, k_past, v_past, lam), attn_subln[l], lam_init)
        qr = rotary(q_r, pos_s)
        kr = rotary(k_r, pos_s) * (DK_R ** -0.5)
        o_r, s_s = retention_chunk(qr, kr, v_r.astype(jnp.float32), state_ret[l].astype(jnp.float32), log_gamma)
        y_r = ret_subnorm(o_r, ret_subln[l], g_r)
        h_s = block_output(h_s, y_a, y_r, g_a, g_b, w_branch[l], w_out[l], norm_mlp[l], w_ff1[l], w_ff2[l])
        k_ss.append(k_a)
        v_ss.append(v_a)
        s_ss.append(s_s)

    y_prompt = rmsnorm(h_p[:, N_META:], norm_final)
    y_sample = rmsnorm(h_s, norm_final)
    return (y_prompt, y_sample, jnp.stack(k_ps), jnp.stack(v_ps), jnp.stack(s_ps),
            jnp.stack(k_ss), jnp.stack(v_ss), jnp.stack(s_ss))
```

```python
import functools
import math

import jax
import jax.numpy as jnp
from jax import lax
from jax.experimental import pallas as pl
from jax.experimental.pallas import tpu as pltpu

F32 = jnp.float32
BF16 = jnp.bfloat16

EPS = 1e-6
N_META = 16
H_A = 4
DH_A = 128
HW_A = 2 * DH_A
W_A = H_A * HW_A
H_R = 8
DK_R = 128
DV_R = 128
W_R = H_R * DV_R
CHUNK = 128
PAGE = 128
ROPE_BASE = 10000.0
LAM_INIT = 0.8 - 0.6 * math.exp(-0.3 * 0)
SCALE_A = DH_A ** -0.5
SCALE_R = DK_R ** -0.5
COL = 1024
C_QA, C_KA, C_VA, C_QR, C_KR, C_VR, C_GR, C_GA = 0, 1, 2, 3, 4, 5, 6, 7

NT_DIMS = (((1,), (1,)), ((), ()))
MIB = 1024 * 1024


def _cparams(sem, vmem_mib):
    return pltpu.CompilerParams(dimension_semantics=sem, vmem_limit_bytes=vmem_mib * MIB)


def _rms(x, gain):
    ms = jnp.mean(x * x, axis=-1, keepdims=True)
    return (x * lax.rsqrt(ms + EPS)) * gain


def _inproj_body(x_ref, g_ref, w_ref, z_ref, a_scr):
    @pl.when(pl.program_id(1) == 0)
    def _():
        a_scr[...] = _rms(x_ref[...], g_ref[...]).astype(BF16)

    z_ref[...] = jnp.dot(a_scr[...], w_ref[...], preferred_element_type=F32)


def _in_project(x, gain, w_bf, tm):
    m, d = x.shape
    n = w_bf.shape[1]
    return pl.pallas_call(
        _inproj_body,
        grid=(m // tm, n // COL),
        in_specs=[pl.BlockSpec((tm, d), lambda i, j: (i, 0)),
                  pl.BlockSpec((1, d), lambda i, j: (0, 0)),
                  pl.BlockSpec((d, COL), lambda i, j: (0, j))],
        out_specs=pl.BlockSpec((tm, COL), lambda i, j: (i, j)),
        out_shape=jax.ShapeDtypeStruct((m, n), F32),
        scratch_shapes=[pltpu.VMEM((tm, d), BF16)],
        compiler_params=_cparams(("parallel", "arbitrary"), 48),
        name="in_project",
    )(x, gain, w_bf)


def _lambda(lq1, lk1, lq2, lk2):
    return (jnp.exp(jnp.sum(lq1[...] * lk1[...], axis=-1, keepdims=True))
            - jnp.exp(jnp.sum(lq2[...] * lk2[...], axis=-1, keepdims=True)) + LAM_INIT)


def _flash_body(q_ref, k_ref, v_ref, km_ref, vm_ref, lq1, lk1, lq2, lk2, gain_ref, o_ref,
                m_scr, l_scr, acc_scr, *, tq):
    qi = pl.program_id(2)
    q = (q_ref[...] * SCALE_A).astype(BF16)

    def scores(kblk):
        kb = kblk.astype(BF16)
        return (lax.dot_general(q[:, :DH_A], kb[:, :DH_A], NT_DIMS, preferred_element_type=F32),
                lax.dot_general(q[:, DH_A:], kb[:, DH_A:], NT_DIMS, preferred_element_type=F32))

    def first(i, s, vb):
        m = jnp.max(s, axis=-1, keepdims=True)
        p = jnp.exp(s - m)
        m_scr[i] = m
        l_scr[i] = jnp.sum(p, axis=-1, keepdims=True)
        acc_scr[i] = jnp.dot(p.astype(BF16), vb, preferred_element_type=F32)

    def update(i, s, vb):
        m_old = m_scr[i]
        m_new = jnp.maximum(m_old, jnp.max(s, axis=-1, keepdims=True))
        alpha = jnp.exp(m_old - m_new)
        p = jnp.exp(s - m_new)
        m_scr[i] = m_new
        l_scr[i] = alpha * l_scr[i] + jnp.sum(p, axis=-1, keepdims=True)
        acc_scr[i] = alpha * acc_scr[i] + jnp.dot(p.astype(BF16), vb, preferred_element_type=F32)

    s1, s2 = scores(km_ref[...])
    meta_ok = lax.broadcasted_iota(jnp.int32, s1.shape, 1) < N_META
    vmb = vm_ref[...].astype(BF16)
    first(0, jnp.where(meta_ok, s1, -jnp.inf), vmb)
    first(1, jnp.where(meta_ok, s2, -jnp.inf), vmb)

    def kv_block(j):
        r0 = pl.multiple_of(j * tq, tq)
        return k_ref[pl.ds(r0, tq), :], v_ref[pl.ds(r0, tq), :].astype(BF16)

    def below_diag(j, carry):
        kblk, vb = kv_block(j)
        t1, t2 = scores(kblk)
        update(0, t1, vb)
        update(1, t2, vb)
        return carry

    lax.fori_loop(0, qi, below_diag, 0)

    kblk, vb = kv_block(qi)
    t1, t2 = scores(kblk)
    causal = (lax.broadcasted_iota(jnp.int32, t1.shape, 1)
              <= lax.broadcasted_iota(jnp.int32, t1.shape, 0))
    update(0, jnp.where(causal, t1, -jnp.inf), vb)
    update(1, jnp.where(causal, t2, -jnp.inf), vb)

    lam = _lambda(lq1, lk1, lq2, lk2)
    o = acc_scr[0] / l_scr[0] - lam * (acc_scr[1] / l_scr[1])
    o_ref[...] = (_rms(o, gain_ref[...]) * (1.0 - LAM_INIT)).astype(o_ref.dtype)


def _flash_diff(z3, km_pad, vm_pad, lams, gain, tq):
    b, seq, _ = z3.shape
    body = functools.partial(_flash_body, tq=tq)
    vec = pl.BlockSpec((1, DH_A), lambda bi, h, qi: (0, 0))
    return pl.pallas_call(
        body,
        grid=(b, H_A, seq // tq),
        in_specs=[pl.BlockSpec((None, tq, HW_A), lambda bi, h, qi: (bi, qi, h)),
                  pl.BlockSpec((None, seq, HW_A), lambda bi, h, qi: (bi, 0, C_KA * H_A + h)),
                  pl.BlockSpec((None, seq, HW_A), lambda bi, h, qi: (bi, 0, C_VA * H_A + h)),
                  pl.BlockSpec((CHUNK, HW_A), lambda bi, h, qi: (0, h)),
                  pl.BlockSpec((CHUNK, HW_A), lambda bi, h, qi: (0, h)),
                  vec, vec, vec, vec,
                  pl.BlockSpec((1, HW_A), lambda bi, h, qi: (0, h))],
        out_specs=pl.BlockSpec((None, tq, HW_A), lambda bi, h, qi: (bi, qi, h)),
        out_shape=jax.ShapeDtypeStruct((b, seq, W_A), BF16),
        scratch_shapes=[pltpu.VMEM((2, tq, 1), F32), pltpu.VMEM((2, tq, 1), F32),
                        pltpu.VMEM((2, tq, HW_A), F32)],
        compiler_params=_cparams(("parallel", "parallel", "arbitrary"), 40),
        name="flash_diff",
    )(z3, z3, z3, km_pad, vm_pad, *lams, gain)


def _rotate(x, cos, sin_signed):
    return x * cos + pltpu.roll(x, DK_R // 2, 1) * sin_signed


def _group_norm_gate(o, gain, gate):
    mu = jnp.mean(o, axis=-1, keepdims=True)
    d = o - mu
    var = jnp.mean(d * d, axis=-1, keepdims=True)
    return (gate * jax.nn.sigmoid(gate)) * ((d * lax.rsqrt(var + EPS)) * gain)


def _ret_body(q_ref, k_ref, v_ref, g_ref, km_ref, vm_ref, cos_ref, sin_ref, cosm_ref, sinm_ref,
              dmat_ref, dq_ref, dk_ref, gc_ref, gain_ref, y_ref, sout_ref, s_scr):
    c = pl.program_id(1)
    hs = lambda h: slice(h * DK_R, (h + 1) * DK_R)

    @pl.when(c == 0)
    def _():
        cosm, sinm = cosm_ref[...], sinm_ref[...]
        for h in range(H_R):
            kr = _rotate(km_ref[:, hs(h)], cosm, sinm) * SCALE_R
            kd = (kr * dk_ref[:, hs(h)]).T.astype(BF16)
            s_scr[h] = jnp.dot(kd, vm_ref[:, hs(h)].astype(BF16), preferred_element_type=F32)

    cos, sin = cos_ref[...], sin_ref[...]
    for h in range(H_R):
        qr = _rotate(q_ref[:, hs(h)], cos, sin)
        kr = _rotate(k_ref[:, hs(h)], cos, sin) * SCALE_R
        vb = v_ref[:, hs(h)].astype(BF16)
        qb = qr.astype(BF16)
        inner = lax.dot_general(qb, kr.astype(BF16), NT_DIMS, preferred_element_type=F32) * dmat_ref[h]
        s = s_scr[h]
        o = (jnp.dot(inner.astype(BF16), vb, preferred_element_type=F32)
             + jnp.dot(qb, s.astype(BF16), preferred_element_type=F32) * dq_ref[:, hs(h)])
        kd = (kr * dk_ref[:, hs(h)]).T.astype(BF16)
        s_scr[h] = gc_ref[h] * s + jnp.dot(kd, vb, preferred_element_type=F32)
        y_ref[:, hs(h)] = _group_norm_gate(o, gain_ref[:, hs(h)], g_ref[:, hs(h)]).astype(y_ref.dtype)

    @pl.when(c == pl.num_programs(1) - 1)
    def _():
        sout_ref[...] = s_scr[...]


def _retention(z3, km_pad, vm_pad, tabs, gain):
    b, seq, _ = z3.shape
    cos_t, sin_t, dmat, dq, dk, gc = tabs
    row = lambda col: pl.BlockSpec((None, CHUNK, COL), lambda bi, c: (bi, c, col))
    fixed2 = lambda shape: pl.BlockSpec(shape, lambda bi, c: (0, 0))
    fixed3 = lambda shape: pl.BlockSpec(shape, lambda bi, c: (0, 0, 0))
    return pl.pallas_call(
        _ret_body,
        grid=(b, seq // CHUNK),
        in_specs=[row(C_QR), row(C_KR), row(C_VR), row(C_GR),
                  fixed2((CHUNK, COL)), fixed2((CHUNK, COL)),
                  pl.BlockSpec((CHUNK, DK_R), lambda bi, c: (c + 1, 0)),
                  pl.BlockSpec((CHUNK, DK_R), lambda bi, c: (c + 1, 0)),
                  fixed2((CHUNK, DK_R)), fixed2((CHUNK, DK_R)),
                  fixed3((H_R, CHUNK, CHUNK)), fixed2((CHUNK, W_R)), fixed2((CHUNK, W_R)),
                  fixed3((H_R, 1, DV_R)), fixed2((1, W_R))],
        out_specs=[pl.BlockSpec((None, CHUNK, W_R), lambda bi, c: (bi, c, 0)),
                   pl.BlockSpec((None, H_R, DK_R, DV_R), lambda bi, c: (bi, 0, 0, 0))],
        out_shape=[jax.ShapeDtypeStruct((b, seq, W_R), BF16),
                   jax.ShapeDtypeStruct((b, H_R, DK_R, DV_R), F32)],
        scratch_shapes=[pltpu.VMEM((H_R, DK_R, DV_R), F32)],
        compiler_params=_cparams(("parallel", "arbitrary"), 32),
        name="retention",
    )(z3, z3, z3, z3, km_pad, vm_pad, cos_t, sin_t, cos_t, sin_t, dmat, dq, dk, gc, gain)


def _retention_tables(n_chunks):
    half = DK_R // 2
    inv = 1.0 / (ROPE_BASE ** jnp.linspace(0.0, 1.0, half, dtype=F32))
    pos = jnp.arange(n_chunks * CHUNK, dtype=jnp.int32) - (CHUNK - N_META)
    ang = pos.astype(F32)[:, None] * inv[None, :]
    cos_t = jnp.concatenate([jnp.cos(ang), jnp.cos(ang)], axis=-1)
    sin_t = jnp.concatenate([-jnp.sin(ang), jnp.sin(ang)], axis=-1)
    lg = jnp.log1p(-(2.0 ** (-5.0 - jnp.arange(H_R, dtype=F32))))
    idx = jnp.arange(CHUNK, dtype=F32)
    diff = idx[:, None] - idx[None, :]
    dmat = jnp.where(diff[None] >= 0, jnp.exp(jnp.maximum(diff, 0.0)[None] * lg[:, None, None]), 0.0)
    dq = jnp.repeat(jnp.exp((idx + 1.0)[:, None] * lg[None, :]), DV_R, axis=1)
    dk = jnp.repeat(jnp.exp((CHUNK - 1.0 - idx)[:, None] * lg[None, :]), DK_R, axis=1)
    gc = jnp.broadcast_to(jnp.exp(CHUNK * lg)[:, None, None], (H_R, 1, DV_R))
    return cos_t, sin_t, dmat, dq, dk, gc


def _branch_body(ya_ref, yr_ref, ga0_ref, ga1_ref, gb0_ref, gb1_ref, x_ref, wa_ref, wb_ref, wo_ref,
                 gm_ref, h_ref, m_ref):
    br_a = jnp.dot(ya_ref[...], wa_ref[...], preferred_element_type=F32)
    br_b = jnp.dot(yr_ref[...], wb_ref[...], preferred_element_type=F32)
    merged = jnp.concatenate(
        [jax.nn.sigmoid(ga0_ref[...]) * br_a[:, :COL] + jax.nn.sigmoid(gb0_ref[...]) * br_b[:, :COL],
         jax.nn.sigmoid(ga1_ref[...]) * br_a[:, COL:] + jax.nn.sigmoid(gb1_ref[...]) * br_b[:, COL:]],
        axis=1)
    h = x_ref[...] + jnp.dot(merged.astype(BF16), wo_ref[...], preferred_element_type=F32)
    h_ref[...] = h
    m_ref[...] = _rms(h, gm_ref[...]).astype(BF16)


def _branch_out(ya, yr, z, x, wa, wb, wo, gm, tm):
    m, d = x.shape
    assert d == 2 * COL
    gate = lambda blk: pl.BlockSpec((tm, COL), lambda i: (i, C_GA + blk))
    resident = lambda shape: pl.BlockSpec(shape, lambda i: (0, 0), pipeline_mode=pl.Buffered(1))
    return pl.pallas_call(
        _branch_body,
        grid=(m // tm,),
        in_specs=[pl.BlockSpec((tm, W_A), lambda i: (i, 0)),
                  pl.BlockSpec((tm, W_R), lambda i: (i, 0)),
                  gate(0), gate(1), gate(2), gate(3),
                  pl.BlockSpec((tm, d), lambda i: (i, 0)),
                  resident((W_A, d)), resident((W_R, d)), resident((d, d)), resident((1, d))],
        out_specs=[pl.BlockSpec((tm, d), lambda i: (i, 0)), pl.BlockSpec((tm, d), lambda i: (i, 0))],
        out_shape=[jax.ShapeDtypeStruct((m, d), F32), jax.ShapeDtypeStruct((m, d), BF16)],
        compiler_params=_cparams(("parallel",), 56),
        name="branch_out",
    )(ya, yr, z, z, z, z, x, wa, wb, wo, gm)


def _ffn_body(m_ref, h_ref, w1_ref, w2_ref, gf_ref, y_ref, acc_scr):
    f = pl.program_id(1)
    u = jnp.maximum(jnp.dot(m_ref[...], w1_ref[...], preferred_element_type=F32), 0.0)
    part = jnp.dot((u * u).astype(BF16), w2_ref[...], preferred_element_type=F32)

    @pl.when(f == 0)
    def _():
        acc_scr[...] = part

    @pl.when(f > 0)
    def _():
        acc_scr[...] += part

    @pl.when(f == pl.num_programs(1) - 1)
    def _():
        y_ref[...] = _rms(h_ref[...] + acc_scr[...], gf_ref[...])


def _ffn(mn, h, w1, w2, gf, tm, tf):
    m, d = h.shape
    dff = w1.shape[1]
    return pl.pallas_call(
        _ffn_body,
        grid=(m // tm, dff // tf),
        in_specs=[pl.BlockSpec((tm, d), lambda i, f: (i, 0)),
                  pl.BlockSpec((tm, d), lambda i, f: (i, 0)),
                  pl.BlockSpec((d, tf), lambda i, f: (0, f)),
                  pl.BlockSpec((tf, d), lambda i, f: (f, 0)),
                  pl.BlockSpec((1, d), lambda i, f: (0, 0))],
        out_specs=pl.BlockSpec((tm, d), lambda i, f: (i, 0)),
        out_shape=jax.ShapeDtypeStruct((m, d), F32),
        scratch_shapes=[pltpu.VMEM((tm, d), F32)],
        compiler_params=_cparams(("parallel", "arbitrary"), 48),
        name="ffn",
    )(mn, h, w1, w2, gf)


PAGES_PER_STEP = 8
STAT_ROWS = 512


def _decode_body(pt_ref, qbd_ref, kn_ref, vn_ref, exp_ref, lq1, lk1, lq2, lk2, gain_ref, *rest, n_keys):
    g_pages = PAGES_PER_STEP
    k_refs = rest[:g_pages]
    v_refs = rest[g_pages:2 * g_pages]
    o_ref = rest[2 * g_pages]
    s_scr, acc_scr, c_scr = rest[2 * g_pages + 1:]
    phase = pl.program_id(1)
    g = pl.program_id(2)
    n_groups = pl.num_programs(2)
    n_stat = n_keys // STAT_ROWS

    @pl.when(phase == 0)
    def _():
        qbd = qbd_ref[...]
        for k in range(g_pages):
            r0 = pl.multiple_of((g * g_pages + k) * PAGE, PAGE)
            s_scr[pl.ds(r0, PAGE), :] = jnp.dot(k_refs[k][...].astype(BF16), qbd,
                                                preferred_element_type=F32)

    @pl.when((phase == 1) & (g == 0))
    def _():
        kn = jnp.broadcast_to(kn_ref[...], (8, W_A)).astype(BF16)
        s_new = jnp.dot(kn, qbd_ref[...], preferred_element_type=F32)[0:1]

        def col_max(i, m):
            r0 = pl.multiple_of(i * STAT_ROWS, STAT_ROWS)
            return jnp.maximum(m, jnp.max(s_scr[pl.ds(r0, STAT_ROWS), :], axis=0, keepdims=True))

        m = lax.fori_loop(0, n_stat, col_max, s_new)

        def col_exp(i, l):
            r0 = pl.multiple_of(i * STAT_ROWS, STAT_ROWS)
            p = jnp.exp(s_scr[pl.ds(r0, STAT_ROWS), :] - m)
            s_scr[pl.ds(r0, STAT_ROWS), :] = p
            return l + jnp.sum(p, axis=0, keepdims=True)

        p_new = jnp.exp(s_new - m)
        l = lax.fori_loop(0, n_stat, col_exp, p_new)
        col = lax.broadcasted_iota(jnp.int32, l.shape, 1)
        lam = _lambda(lq1, lk1, lq2, lk2)
        sign = jnp.where(col % 2 == 0, 1.0, -lam)
        coef = jnp.where(col < 2 * H_A, sign / l, 0.0)
        c_scr[...] = coef
        w_new = jnp.broadcast_to(p_new * coef, (8, PAGE)).astype(BF16)
        w_exp = jnp.dot(w_new, exp_ref[...], preferred_element_type=F32)
        first_row = lax.broadcasted_iota(jnp.int32, w_exp.shape, 0) == 0
        acc_scr[...] = jnp.where(first_row, w_exp * vn_ref[...], 0.0)

    @pl.when(phase == 1)
    def _():
        coef = c_scr[...]
        expand = exp_ref[...]
        acc = acc_scr[...]
        for k in range(g_pages):
            r0 = pl.multiple_of((g * g_pages + k) * PAGE, PAGE)
            p = (s_scr[pl.ds(r0, PAGE), :] * coef).astype(BF16)
            w = jnp.dot(p, expand, preferred_element_type=F32)
            acc = acc + jnp.sum((w * v_refs[k][...]).reshape(PAGE // 8, 8, W_A), axis=0)
        acc_scr[...] = acc

    @pl.when((phase == 1) & (g == n_groups - 1))
    def _():
        o = jnp.sum(acc_scr[...], axis=0, keepdims=True)
        for h in range(H_A):
            sl = slice(h * HW_A, (h + 1) * HW_A)
            o_ref[:, sl] = (_rms(o[:, sl], gain_ref[:, sl]) * (1.0 - LAM_INIT)).astype(o_ref.dtype)


def _decode_attn(page_table, qbd, k_new, v_new, cache_k, cache_v, expand, lams, gain):
    db, n_pages = page_table.shape
    g_pages = PAGES_PER_STEP
    n_groups = n_pages // g_pages
    n_keys = n_pages * PAGE
    body = functools.partial(_decode_body, n_keys=n_keys)

    def k_spec(k):
        return pl.BlockSpec(
            (None, PAGE, W_A),
            lambda b, ph, g, pt: (pt[b, jnp.where(ph == 0, g, n_groups - 1) * g_pages + k], 0, 0))

    def v_spec(k):
        return pl.BlockSpec(
            (None, PAGE, W_A),
            lambda b, ph, g, pt: (pt[b, jnp.where(ph == 1, g, 0) * g_pages + k], 0, 0))

    per_b = lambda shape: pl.BlockSpec(shape, lambda b, ph, g, pt: (b, 0, 0))
    fixed = lambda shape: pl.BlockSpec(shape, lambda b, ph, g, pt: (0, 0))
    grid_spec = pltpu.PrefetchScalarGridSpec(
        num_scalar_prefetch=1,
        grid=(db, 2, n_groups),
        in_specs=[per_b((None, W_A, PAGE)), per_b((None, 1, W_A)), per_b((None, 1, W_A)),
                  fixed((PAGE, W_A)), fixed((1, DH_A)), fixed((1, DH_A)), fixed((1, DH_A)),
                  fixed((1, DH_A)), fixed((1, W_A))]
                 + [k_spec(k) for k in range(g_pages)] + [v_spec(k) for k in range(g_pages)],
        out_specs=per_b((None, 1, W_A)),
        scratch_shapes=[pltpu.VMEM((n_keys, PAGE), F32), pltpu.VMEM((8, W_A), F32),
                        pltpu.VMEM((1, PAGE), F32)],
    )
    return pl.pallas_call(
        body,
        grid_spec=grid_spec,
        out_shape=jax.ShapeDtypeStruct((db, 1, W_A), BF16),
        compiler_params=_cparams(("arbitrary", "arbitrary", "arbitrary"), 48),
        name="decode_attn",
    )(page_table, qbd, k_new, v_new, expand, *lams, gain,
      *([cache_k] * g_pages), *([cache_v] * g_pages))


def _decode_ret_body(q_ref, k_ref, v_ref, g_ref, s_ref, cos_ref, sin_ref, gam_ref, gain_ref,
                     y_ref, sout_ref):
    cos, sin = cos_ref[...], sin_ref[...]
    for h in range(H_R):
        sl = slice(h * DK_R, (h + 1) * DK_R)
        qr = _rotate(q_ref[:, sl], cos, sin)
        kr = _rotate(k_ref[:, sl], cos, sin) * SCALE_R
        v = v_ref[:, sl]
        gam = gam_ref[h]
        s = s_ref[h]
        qk = jnp.sum(qr * kr, axis=-1, keepdims=True)
        qs = jnp.dot(jnp.broadcast_to(qr, (8, DK_R)).astype(BF16), s.astype(BF16),
                     preferred_element_type=F32)[0:1]
        o = qk * v + qs * gam
        k_col = jnp.broadcast_to(kr, (DK_R, DK_R)).T
        sout_ref[h] = gam * s + k_col * v
        y_ref[:, sl] = _group_norm_gate(o, gain_ref[:, sl], g_ref[:, sl]).astype(y_ref.dtype)


def _decode_ret(q, k, v, g, state, cos_s, sin_s, gam, gain):
    db = q.shape[0]
    row = pl.BlockSpec((None, 1, W_R), lambda b: (b, 0, 0))
    st = pl.BlockSpec((None, H_R, DK_R, DV_R), lambda b: (b, 0, 0, 0))
    return pl.pallas_call(
        _decode_ret_body,
        grid=(db,),
        in_specs=[row, row, row, row, st,
                  pl.BlockSpec((1, DK_R), lambda b: (0, 0)), pl.BlockSpec((1, DK_R), lambda b: (0, 0)),
                  pl.BlockSpec((H_R, 1, DV_R), lambda b: (0, 0, 0)),
                  pl.BlockSpec((1, W_R), lambda b: (0, 0))],
        out_specs=[row, st],
        out_shape=[jax.ShapeDtypeStruct((db, 1, W_R), BF16),
                   jax.ShapeDtypeStruct((db, H_R, DK_R, DV_R), F32)],
        compiler_params=_cparams(("parallel",), 16),
        name="decode_ret",
    )(q, k, v, g, state, cos_s, sin_s, gam, gain)


def _row_tile(m, pref):
    t = min(m, pref)
    assert m % t == 0
    return t


def kernel(x_prompt, x_sample, cache_k, cache_v, state_ret, page_table, meta_tokens, norm_mix, w_in,
           lambda_q1, lambda_k1, lambda_q2, lambda_k2, attn_subln, ret_subln, w_branch, w_out,
           norm_mlp, w_ff1, w_ff2, norm_final):
    b, seq, d = x_prompt.shape
    db, t_s, _ = x_sample.shape
    assert t_s == 1 and w_in.shape[0] == 1
    n_pages = page_table.shape[1]
    past_len = n_pages * PAGE
    n_pool = cache_k.shape[1]

    w_in_bf = w_in[0].astype(BF16)
    wa_bf = w_branch[0, 0].astype(BF16)
    wb_bf = w_branch[0, 1].astype(BF16)
    wo_bf = w_out[0].astype(BF16)
    w1_bf = w_ff1[0].astype(BF16)
    w2_bf = w_ff2[0].astype(BF16)
    g_mix = norm_mix[0][None, :]
    g_mlp = norm_mlp[0][None, :]
    g_fin = norm_final[None, :]
    g_attn = attn_subln[0][None, :]
    g_ret = ret_subln[0][None, :]
    lams = (lambda_q1, lambda_k1, lambda_q2, lambda_k2)

    xp = x_prompt.reshape(b * seq, d)
    z = _in_project(xp, g_mix, w_in_bf, _row_tile(b * seq, 512))
    small_rows = -(-(db + N_META) // 16) * 16
    xs = jnp.concatenate([x_sample.reshape(db, d), meta_tokens.astype(F32),
                          jnp.zeros((small_rows - db - N_META, d), F32)], axis=0)
    zs = _in_project(xs, g_mix, w_in_bf, small_rows)
    z3 = z.reshape(b, seq, -1)

    def cols(arr, c0, n=1):
        return arr[:, c0 * COL:(c0 + n) * COL]

    z_meta = zs[db:db + N_META]
    k_meta, v_meta = cols(z_meta, C_KA), cols(z_meta, C_VA)
    back_pad = ((0, CHUNK - N_META), (0, 0))
    front_pad = ((CHUNK - N_META, 0), (0, 0))

    y_a = _flash_diff(z3, jnp.pad(k_meta, back_pad), jnp.pad(v_meta, back_pad), lams, g_attn,
                      _row_tile(seq, 256))

    n_chunks = seq // CHUNK + 1
    tabs = _retention_tables(n_chunks)
    y_r, s_prompt = _retention(z3, jnp.pad(cols(z_meta, C_KR), front_pad),
                               jnp.pad(cols(z_meta, C_VR), front_pad), tabs, g_ret)

    h1, mn = _branch_out(y_a.reshape(b * seq, W_A), y_r.reshape(b * seq, W_R), z, xp,
                         wa_bf, wb_bf, wo_bf, g_mlp, _row_tile(b * seq, 256))
    y_prompt = _ffn(mn, h1, w1_bf, w2_bf, g_fin, _row_tile(b * seq, 512), 512).reshape(b, seq, d)

    z_s = zs[:db]
    q_s = cols(z_s, C_QA) * SCALE_A
    r = jnp.arange(W_A)
    onehot = (r[:, None] // DH_A == jnp.arange(PAGE)[None, :])
    qbd = jnp.where(onehot[None], q_s[:, :, None], 0.0).astype(BF16)
    expand = (jnp.arange(PAGE)[:, None] // 2 == r[None, :] // HW_A).astype(BF16)
    k_s, v_s = cols(z_s, C_KA), cols(z_s, C_VA)
    ya_s = _decode_attn(page_table, qbd, k_s[:, None, :], v_s[:, None, :],
                        cache_k[0].reshape(n_pool, PAGE, W_A), cache_v[0].reshape(n_pool, PAGE, W_A),
                        expand, lams, g_attn)

    half = DK_R // 2
    inv = 1.0 / (ROPE_BASE ** jnp.linspace(0.0, 1.0, half, dtype=F32))
    ang = jnp.full((1,), past_len, jnp.int32).astype(F32)[:, None] * inv[None, :]
    cos_s = jnp.concatenate([jnp.cos(ang), jnp.cos(ang)], axis=-1)
    sin_s = jnp.concatenate([-jnp.sin(ang), jnp.sin(ang)], axis=-1)
    lg = jnp.log1p(-(2.0 ** (-5.0 - jnp.arange(H_R, dtype=F32))))
    gam = jnp.broadcast_to(jnp.exp(lg)[:, None, None], (H_R, 1, DV_R))
    yr_s, s_sample = _decode_ret(cols(z_s, C_QR)[:, None, :], cols(z_s, C_KR)[:, None, :],
                                 cols(z_s, C_VR)[:, None, :], cols(z_s, C_GR)[:, None, :],
                                 state_ret[0], cos_s, sin_s, gam, g_ret)

    xs_rows = x_sample.reshape(db, d)
    h1_s, mn_s = _branch_out(ya_s.reshape(db, W_A), yr_s.reshape(db, W_R), z_s, xs_rows,
                             wa_bf, wb_bf, wo_bf, g_mlp, db)
    y_sample = _ffn(mn_s, h1_s, w1_bf, w2_bf, g_fin, db, 512).reshape(db, 1, d)

    def with_meta(main, meta):
        full = jnp.concatenate([jnp.broadcast_to(meta[None], (b, N_META, W_A)),
                                main.reshape(b, seq, W_A)], axis=1)
        return full.reshape(1, b, seq + N_META, H_A, HW_A)

    k_prompt = with_meta(cols(z, C_KA), k_meta)
    v_prompt = with_meta(cols(z, C_VA), v_meta)
    return (y_prompt, y_sample, k_prompt, v_prompt, s_prompt[None],
            k_s.reshape(1, db, 1, H_A, HW_A), v_s.reshape(1, db, 1, H_A, HW_A), s_sample[None])
```

```python
import functools
import math

import jax
import jax.numpy as jnp
from jax import lax
from jax.experimental import pallas as pl
from jax.experimental.pallas import tpu as pltpu

F32 = jnp.float32
BF16 = jnp.bfloat16

EPS = 1e-6
N_META = 16
H_A = 4
DH_A = 128
HW_A = 2 * DH_A
W_A = H_A * HW_A
H_R = 8
DK_R = 128
DV_R = 128
W_R = H_R * DV_R
CHUNK = 128
PAGE = 128
ROPE_BASE = 10000.0
LAM_INIT = 0.8 - 0.6 * math.exp(-0.3 * 0)
SCALE_A = DH_A ** -0.5
SCALE_R = DK_R ** -0.5
COL = 1024
C_QA, C_KA, C_VA, C_QR, C_KR, C_VR, C_GR, C_GA = 0, 1, 2, 3, 4, 5, 6, 7

NT_DIMS = (((1,), (1,)), ((), ()))
MIB = 1024 * 1024


def _cparams(sem, vmem_mib):
    return pltpu.CompilerParams(dimension_semantics=sem, vmem_limit_bytes=vmem_mib * MIB)


def _rms(x, gain):
    ms = jnp.mean(x * x, axis=-1, keepdims=True)
    return (x * lax.rsqrt(ms + EPS)) * gain


def _inproj_body(x_ref, g_ref, w_ref, z_ref, a_scr):
    @pl.when(pl.program_id(1) == 0)
    def _():
        a_scr[...] = _rms(x_ref[...], g_ref[...]).astype(BF16)

    z_ref[...] = jnp.dot(a_scr[...], w_ref[...], preferred_element_type=F32)


def _in_project(x, gain, w_bf, tm):
    m, d = x.shape
    n = w_bf.shape[1]
    return pl.pallas_call(
        _inproj_body,
        grid=(m // tm, n // COL),
        in_specs=[pl.BlockSpec((tm, d), lambda i, j: (i, 0)),
                  pl.BlockSpec((1, d), lambda i, j: (0, 0)),
                  pl.BlockSpec((d, COL), lambda i, j: (0, j))],
        out_specs=pl.BlockSpec((tm, COL), lambda i, j: (i, j)),
        out_shape=jax.ShapeDtypeStruct((m, n), F32),
        scratch_shapes=[pltpu.VMEM((tm, d), BF16)],
        compiler_params=_cparams(("parallel", "arbitrary"), 48),
        name="in_project",
    )(x, gain, w_bf)


def _lambda(lq1, lk1, lq2, lk2):
    return (jnp.exp(jnp.sum(lq1[...] * lk1[...], axis=-1, keepdims=True))
            - jnp.exp(jnp.sum(lq2[...] * lk2[...], axis=-1, keepdims=True)) + LAM_INIT)


def _flash_body(q_ref, k_ref, v_ref, km_ref, vm_ref, lq1, lk1, lq2, lk2, gain_ref, o_ref,
                m_scr, l_scr, acc_scr, *, tq):
    qi = pl.program_id(2)
    q = (q_ref[...] * SCALE_A).astype(BF16)

    def scores(kblk):
        kb = kblk.astype(BF16)
        return (lax.dot_general(q[:, :DH_A], kb[:, :DH_A], NT_DIMS, preferred_element_type=F32),
                lax.dot_general(q[:, DH_A:], kb[:, DH_A:], NT_DIMS, preferred_element_type=F32))

    def lane_fold(p):
        return functools.reduce(jnp.add, [p[:, i:i + DH_A] for i in range(0, p.shape[1], DH_A)])

    def first(i, s, vb):
        m = jnp.max(s, axis=-1, keepdims=True)
        p = jnp.exp(s - m)
        m_scr[i] = m
        l_scr[i] = lane_fold(p)
        acc_scr[i] = jnp.dot(p.astype(BF16), vb, preferred_element_type=F32)

    def update(i, s, vb):
        m_old = m_scr[i]
        m_new = jnp.maximum(m_old, jnp.max(s, axis=-1, keepdims=True))
        alpha = jnp.exp(m_old - m_new)
        p = jnp.exp(s - m_new)
        m_scr[i] = m_new
        l_scr[i] = alpha * l_scr[i] + lane_fold(p)
        acc_scr[i] = alpha * acc_scr[i] + jnp.dot(p.astype(BF16), vb, preferred_element_type=F32)

    s1, s2 = scores(km_ref[...])
    meta_ok = lax.broadcasted_iota(jnp.int32, s1.shape, 1) < N_META
    vmb = vm_ref[...].astype(BF16)
    first(0, jnp.where(meta_ok, s1, -jnp.inf), vmb)
    first(1, jnp.where(meta_ok, s2, -jnp.inf), vmb)

    def kv_block(r0, rows):
        return k_ref[pl.ds(r0, rows), :], v_ref[pl.ds(r0, rows), :].astype(BF16)

    def attend(r0, rows):
        kblk, vb = kv_block(r0, rows)
        t1, t2 = scores(kblk)
        update(0, t1, vb)
        update(1, t2, vb)

    def below_diag(j, carry):
        attend(pl.multiple_of(j * 2 * tq, 2 * tq), 2 * tq)
        return carry

    lax.fori_loop(0, qi // 2, below_diag, 0)

    @pl.when(qi % 2 == 1)
    def _():
        attend(pl.multiple_of((qi - 1) * tq, tq), tq)

    kblk, vb = kv_block(pl.multiple_of(qi * tq, tq), tq)
    t1, t2 = scores(kblk)
    causal = (lax.broadcasted_iota(jnp.int32, t1.shape, 1)
              <= lax.broadcasted_iota(jnp.int32, t1.shape, 0))
    update(0, jnp.where(causal, t1, -jnp.inf), vb)
    update(1, jnp.where(causal, t2, -jnp.inf), vb)

    lam = _lambda(lq1, lk1, lq2, lk2)
    l1 = jnp.sum(l_scr[0], axis=-1, keepdims=True)
    l2 = jnp.sum(l_scr[1], axis=-1, keepdims=True)
    o = acc_scr[0] / l1 - lam * (acc_scr[1] / l2)
    o_ref[...] = (_rms(o, gain_ref[...]) * (1.0 - LAM_INIT)).astype(o_ref.dtype)


def _flash_diff(z3, km_pad, vm_pad, lams, gain, tq):
    b, seq, _ = z3.shape
    body = functools.partial(_flash_body, tq=tq)
    vec = pl.BlockSpec((1, DH_A), lambda bi, h, qi: (0, 0))
    return pl.pallas_call(
        body,
        grid=(b, H_A, seq // tq),
        in_specs=[pl.BlockSpec((None, tq, HW_A), lambda bi, h, qi: (bi, qi, h)),
                  pl.BlockSpec((None, seq, HW_A), lambda bi, h, qi: (bi, 0, C_KA * H_A + h)),
                  pl.BlockSpec((None, seq, HW_A), lambda bi, h, qi: (bi, 0, C_VA * H_A + h)),
                  pl.BlockSpec((CHUNK, HW_A), lambda bi, h, qi: (0, h)),
                  pl.BlockSpec((CHUNK, HW_A), lambda bi, h, qi: (0, h)),
                  vec, vec, vec, vec,
                  pl.BlockSpec((1, HW_A), lambda bi, h, qi: (0, h))],
        out_specs=pl.BlockSpec((None, tq, HW_A), lambda bi, h, qi: (bi, qi, h)),
        out_shape=jax.ShapeDtypeStruct((b, seq, W_A), BF16),
        scratch_shapes=[pltpu.VMEM((2, tq, 1), F32), pltpu.VMEM((2, tq, DH_A), F32),
                        pltpu.VMEM((2, tq, HW_A), F32)],
        compiler_params=_cparams(("parallel", "parallel", "arbitrary"), 40),
        name="flash_diff",
    )(z3, z3, z3, km_pad, vm_pad, *lams, gain)


def _rotate(x, cos, sin_signed):
    return x * cos + pltpu.roll(x, DK_R // 2, 1) * sin_signed


def _group_norm_gate(o, gain, gate):
    mu = jnp.mean(o, axis=-1, keepdims=True)
    d = o - mu
    var = jnp.mean(d * d, axis=-1, keepdims=True)
    return (gate * jax.nn.sigmoid(gate)) * ((d * lax.rsqrt(var + EPS)) * gain)


def _ret_body(q_ref, k_ref, v_ref, g_ref, km_ref, vm_ref, cos_ref, sin_ref, cosm_ref, sinm_ref,
              dmat_ref, dq_ref, dk_ref, gc_ref, gain_ref, y_ref, sout_ref, s_scr):
    c = pl.program_id(1)
    hs = lambda h: slice(h * DK_R, (h + 1) * DK_R)

    @pl.when(c == 0)
    def _():
        cosm, sinm = cosm_ref[...], sinm_ref[...]
        for h in range(H_R):
            kr = _rotate(km_ref[:, hs(h)], cosm, sinm) * SCALE_R
            kd = (kr * dk_ref[:, hs(h)]).T.astype(BF16)
            s_scr[h] = jnp.dot(kd, vm_ref[:, hs(h)].astype(BF16), preferred_element_type=F32)

    cos, sin = cos_ref[...], sin_ref[...]
    for h in range(H_R):
        qr = _rotate(q_ref[:, hs(h)], cos, sin)
        kr = _rotate(k_ref[:, hs(h)], cos, sin) * SCALE_R
        vb = v_ref[:, hs(h)].astype(BF16)
        qb = qr.astype(BF16)
        inner = lax.dot_general(qb, kr.astype(BF16), NT_DIMS, preferred_element_type=F32) * dmat_ref[h]
        s = s_scr[h]
        o = (jnp.dot(inner.astype(BF16), vb, preferred_element_type=F32)
             + jnp.dot(qb, s.astype(BF16), preferred_element_type=F32) * dq_ref[:, hs(h)])
        kd = (kr * dk_ref[:, hs(h)]).T.astype(BF16)
        s_scr[h] = gc_ref[h] * s + jnp.dot(kd, vb, preferred_element_type=F32)
        y_ref[:, hs(h)] = _group_norm_gate(o, gain_ref[:, hs(h)], g_ref[:, hs(h)]).astype(y_ref.dtype)

    @pl.when(c == pl.num_programs(1) - 1)
    def _():
        sout_ref[...] = s_scr[...]


def _retention(z3, km_pad, vm_pad, tabs, gain):
    b, seq, _ = z3.shape
    cos_t, sin_t, dmat, dq, dk, gc = tabs
    row = lambda col: pl.BlockSpec((None, CHUNK, COL), lambda bi, c: (bi, c, col))
    fixed2 = lambda shape: pl.BlockSpec(shape, lambda bi, c: (0, 0))
    fixed3 = lambda shape: pl.BlockSpec(shape, lambda bi, c: (0, 0, 0))
    return pl.pallas_call(
        _ret_body,
        grid=(b, seq // CHUNK),
        in_specs=[row(C_QR), row(C_KR), row(C_VR), row(C_GR),
                  fixed2((CHUNK, COL)), fixed2((CHUNK, COL)),
                  pl.BlockSpec((CHUNK, DK_R), lambda bi, c: (c + 1, 0)),
                  pl.BlockSpec((CHUNK, DK_R), lambda bi, c: (c + 1, 0)),
                  fixed2((CHUNK, DK_R)), fixed2((CHUNK, DK_R)),
                  fixed3((H_R, CHUNK, CHUNK)), fixed2((CHUNK, W_R)), fixed2((CHUNK, W_R)),
                  fixed3((H_R, 1, DV_R)), fixed2((1, W_R))],
        out_specs=[pl.BlockSpec((None, CHUNK, W_R), lambda bi, c: (bi, c, 0)),
                   pl.BlockSpec((None, H_R, DK_R, DV_R), lambda bi, c: (bi, 0, 0, 0))],
        out_shape=[jax.ShapeDtypeStruct((b, seq, W_R), BF16),
                   jax.ShapeDtypeStruct((b, H_R, DK_R, DV_R), F32)],
        scratch_shapes=[pltpu.VMEM((H_R, DK_R, DV_R), F32)],
        compiler_params=_cparams(("parallel", "arbitrary"), 32),
        name="retention",
    )(z3, z3, z3, z3, km_pad, vm_pad, cos_t, sin_t, cos_t, sin_t, dmat, dq, dk, gc, gain)


def _retention_tables(n_chunks):
    half = DK_R // 2
    inv = 1.0 / (ROPE_BASE ** jnp.linspace(0.0, 1.0, half, dtype=F32))
    pos = jnp.arange(n_chunks * CHUNK, dtype=jnp.int32) - (CHUNK - N_META)
    ang = pos.astype(F32)[:, None] * inv[None, :]
    cos_t = jnp.concatenate([jnp.cos(ang), jnp.cos(ang)], axis=-1)
    sin_t = jnp.concatenate([-jnp.sin(ang), jnp.sin(ang)], axis=-1)
    lg = jnp.log1p(-(2.0 ** (-5.0 - jnp.arange(H_R, dtype=F32))))
    idx = jnp.arange(CHUNK, dtype=F32)
    diff = idx[:, None] - idx[None, :]
    dmat = jnp.where(diff[None] >= 0, jnp.exp(jnp.maximum(diff, 0.0)[None] * lg[:, None, None]), 0.0)
    dq = jnp.repeat(jnp.exp((idx + 1.0)[:, None] * lg[None, :]), DV_R, axis=1)
    dk = jnp.repeat(jnp.exp((CHUNK - 1.0 - idx)[:, None] * lg[None, :]), DK_R, axis=1)
    gc = jnp.broadcast_to(jnp.exp(CHUNK * lg)[:, None, None], (H_R, 1, DV_R))
    return cos_t, sin_t, dmat, dq, dk, gc


def _branch_body(ya_ref, yr_ref, ga0_ref, ga1_ref, gb0_ref, gb1_ref, x_ref, wa_ref, wb_ref, wo_ref,
                 gm_ref, h_ref, m_ref):
    br_a = jnp.dot(ya_ref[...], wa_ref[...], preferred_element_type=F32)
    br_b = jnp.dot(yr_ref[...], wb_ref[...], preferred_element_type=F32)
    merged = jnp.concatenate(
        [jax.nn.sigmoid(ga0_ref[...]) * br_a[:, :COL] + jax.nn.sigmoid(gb0_ref[...]) * br_b[:, :COL],
         jax.nn.sigmoid(ga1_ref[...]) * br_a[:, COL:] + jax.nn.sigmoid(gb1_ref[...]) * br_b[:, COL:]],
        axis=1)
    h = x_ref[...] + jnp.dot(merged.astype(BF16), wo_ref[...], preferred_element_type=F32)
    h_ref[...] = h
    m_ref[...] = _rms(h, gm_ref[...]).astype(BF16)


def _branch_out(ya, yr, z, x, wa, wb, wo, gm, tm):
    m, d = x.shape
    assert d == 2 * COL
    gate = lambda blk: pl.BlockSpec((tm, COL), lambda i: (i, C_GA + blk))
    resident = lambda shape: pl.BlockSpec(shape, lambda i: (0, 0), pipeline_mode=pl.Buffered(1))
    return pl.pallas_call(
        _branch_body,
        grid=(m // tm,),
        in_specs=[pl.BlockSpec((tm, W_A), lambda i: (i, 0)),
                  pl.BlockSpec((tm, W_R), lambda i: (i, 0)),
                  gate(0), gate(1), gate(2), gate(3),
                  pl.BlockSpec((tm, d), lambda i: (i, 0)),
                  resident((W_A, d)), resident((W_R, d)), resident((d, d)), resident((1, d))],
        out_specs=[pl.BlockSpec((tm, d), lambda i: (i, 0)), pl.BlockSpec((tm, d), lambda i: (i, 0))],
        out_shape=[jax.ShapeDtypeStruct((m, d), F32), jax.ShapeDtypeStruct((m, d), BF16)],
        compiler_params=_cparams(("parallel",), 56),
        name="branch_out",
    )(ya, yr, z, z, z, z, x, wa, wb, wo, gm)


def _ffn_body(m_ref, h_ref, w1_ref, w2_ref, gf_ref, y_ref, acc_scr):
    f = pl.program_id(1)
    u = jnp.maximum(jnp.dot(m_ref[...], w1_ref[...], preferred_element_type=F32), 0.0)
    part = jnp.dot((u * u).astype(BF16), w2_ref[...], preferred_element_type=F32)

    @pl.when(f == 0)
    def _():
        acc_scr[...] = part

    @pl.when(f > 0)
    def _():
        acc_scr[...] += part

    @pl.when(f == pl.num_programs(1) - 1)
    def _():
        y_ref[...] = _rms(h_ref[...] + acc_scr[...], gf_ref[...])


def _ffn(mn, h, w1, w2, gf, tm, tf):
    m, d = h.shape
    dff = w1.shape[1]
    return pl.pallas_call(
        _ffn_body,
        grid=(m // tm, dff // tf),
        in_specs=[pl.BlockSpec((tm, d), lambda i, f: (i, 0)),
                  pl.BlockSpec((tm, d), lambda i, f: (i, 0)),
                  pl.BlockSpec((d, tf), lambda i, f: (0, f)),
                  pl.BlockSpec((tf, d), lambda i, f: (f, 0)),
                  pl.BlockSpec((1, d), lambda i, f: (0, 0))],
        out_specs=pl.BlockSpec((tm, d), lambda i, f: (i, 0)),
        out_shape=jax.ShapeDtypeStruct((m, d), F32),
        scratch_shapes=[pltpu.VMEM((tm, d), F32)],
        compiler_params=_cparams(("parallel", "arbitrary"), 48),
        name="ffn",
    )(mn, h, w1, w2, gf)


PAGES_PER_STEP = 8
PAIRS = PAGE // 2
NCOL = 2 * H_A


def _swap_parity(x):
    return jnp.concatenate([x[:, H_A:], x[:, :H_A]], axis=1)


def _lane_to_sublane(w):
    wide = jnp.broadcast_to(w[:, None, :], (w.shape[0], NCOL, NCOL))
    diag = (lax.broadcasted_iota(jnp.int32, wide.shape, 1)
            == lax.broadcasted_iota(jnp.int32, wide.shape, 2))
    return jnp.sum(jnp.where(diag, wide, 0.0), axis=-1, keepdims=True)


def _page_rows(ref, j):
    return ref[:, :, j * DH_A:(j + 1) * DH_A].reshape(PAIRS, NCOL, DH_A)


def _decode_body(pt_ref, q_ref, kn_ref, vn_ref, lq1, lk1, lq2, lk2, gain_ref, *rest):
    g_pages = PAGES_PER_STEP
    k_refs = rest[:g_pages]
    v_refs = rest[g_pages:2 * g_pages]
    o_ref = rest[2 * g_pages]
    s_scr, m_scr, l_scr, c_scr, acc_scr = rest[2 * g_pages + 1:]
    phase = pl.program_id(1)
    g = pl.program_id(2)
    n_groups = pl.num_programs(2)

    def page_rows0(k):
        return pl.multiple_of((g * g_pages + k) * PAIRS, PAIRS)

    def new_token_scores(j):
        return jnp.sum((kn_ref[j] * q_ref[j])[None], axis=-1)

    @pl.when((phase == 0) & (g == 0))
    def _():
        for j in range(2):
            m_scr[j] = new_token_scores(j)
            l_scr[j] = jnp.zeros((1, NCOL), F32)

    @pl.when(phase == 0)
    def _():
        for j in range(2):
            top = None
            for k in range(g_pages):
                blk = jnp.sum(_page_rows(k_refs[k], j) * q_ref[j][None], axis=-1)
                s_scr[j, pl.ds(page_rows0(k), PAIRS), :] = blk
                top = blk if top is None else jnp.maximum(top, blk)
            m_blk = jnp.max(top, axis=0, keepdims=True)
            m_old = m_scr[j]
            m_new = jnp.maximum(m_old, jnp.maximum(m_blk, _swap_parity(m_blk)))
            l_add = jnp.zeros((PAIRS, NCOL), F32)
            for k in range(g_pages):
                l_add = l_add + jnp.exp(s_scr[j, pl.ds(page_rows0(k), PAIRS), :] - m_new)
            l_scr[j] = l_scr[j] * jnp.exp(m_old - m_new) + jnp.sum(l_add, axis=0, keepdims=True)
            m_scr[j] = m_new

    @pl.when((phase == 1) & (g == 0))
    def _():
        lam = _lambda(lq1, lk1, lq2, lk2)
        w_new = jnp.zeros((1, NCOL), F32)
        for j in range(2):
            p_new = jnp.exp(new_token_scores(j) - m_scr[j])
            l = l_scr[j] + _swap_parity(l_scr[j]) + p_new
            coef = (1.0 if j == 0 else -lam) / l
            c_scr[j] = coef
            w_new = w_new + p_new * coef
        first_key = lax.broadcasted_iota(jnp.int32, w_new.shape, 1) < H_A
        wb = _lane_to_sublane(jnp.where(first_key, w_new, 0.0))
        for j in range(2):
            acc_scr[j] = jnp.sum(wb * vn_ref[j][None], axis=0)

    @pl.when(phase == 1)
    def _():
        acc = [acc_scr[0], acc_scr[1]]
        for k in range(g_pages):
            rows = pl.ds(page_rows0(k), PAIRS)
            w = (jnp.exp(s_scr[0, rows, :] - m_scr[0]) * c_scr[0]
                 + jnp.exp(s_scr[1, rows, :] - m_scr[1]) * c_scr[1])
            wb = _lane_to_sublane(w)
            for j in range(2):
                acc[j] = acc[j] + jnp.sum(wb * _page_rows(v_refs[k], j), axis=0)
        acc_scr[0] = acc[0]
        acc_scr[1] = acc[1]

    @pl.when((phase == 1) & (g == n_groups - 1))
    def _():
        halves = [acc_scr[j][:H_A] + acc_scr[j][H_A:] for j in range(2)]
        o = jnp.concatenate(halves, axis=1)
        o_ref[...] = _rms(o, gain_ref[...]) * (1.0 - LAM_INIT)


def _decode_attn(page_table, q_t, kn_t, vn_t, cache_k, cache_v, lams, gain_heads):
    db, n_pages = page_table.shape
    g_pages = PAGES_PER_STEP
    n_groups = n_pages // g_pages
    n_keys = n_pages * PAGE

    def page_spec(k, active_phase, idle_group):
        def index(b, ph, g, pt):
            grp = jnp.where(ph == active_phase, g, idle_group)
            return (0, pt[b, grp * g_pages + k], 0, 0, 0)
        return pl.BlockSpec((None, None, PAGE, H_A, HW_A), index)

    k_specs = [page_spec(k, 0, n_groups - 1) for k in range(g_pages)]
    v_specs = [page_spec(k, 1, 0) for k in range(g_pages)]
    per_b = lambda shape: pl.BlockSpec(shape, lambda b, ph, g, pt: (b,) + (0,) * (len(shape) - 1))
    fixed = lambda shape: pl.BlockSpec(shape, lambda b, ph, g, pt: (0, 0))
    grid_spec = pltpu.PrefetchScalarGridSpec(
        num_scalar_prefetch=1,
        grid=(db, 2, n_groups),
        in_specs=[per_b((None, 2, NCOL, DH_A)), per_b((None, 2, NCOL, DH_A)), per_b((None, 2, NCOL, DH_A)),
                  fixed((1, DH_A)), fixed((1, DH_A)), fixed((1, DH_A)), fixed((1, DH_A)),
                  fixed((H_A, HW_A))] + k_specs + v_specs,
        out_specs=per_b((None, H_A, HW_A)),
        scratch_shapes=[pltpu.VMEM((2, n_keys // 2, NCOL), F32), pltpu.VMEM((2, 1, NCOL), F32),
                        pltpu.VMEM((2, 1, NCOL), F32), pltpu.VMEM((2, 1, NCOL), F32),
                        pltpu.VMEM((2, NCOL, DH_A), F32)],
    )
    return pl.pallas_call(
        _decode_body,
        grid_spec=grid_spec,
        out_shape=jax.ShapeDtypeStruct((db, H_A, HW_A), F32),
        compiler_params=_cparams(("arbitrary", "arbitrary", "arbitrary"), 48),
        name="decode_attn",
    )(page_table, q_t, kn_t, vn_t, *lams, gain_heads,
      *([cache_k] * g_pages), *([cache_v] * g_pages))


def _decode_ret_body(q_ref, k_ref, v_ref, g_ref, s_ref, cos_ref, sin_ref, gam_ref, gain_ref,
                     y_ref, sout_ref):
    cos, sin = cos_ref[...], sin_ref[...]
    for h in range(H_R):
        sl = slice(h * DK_R, (h + 1) * DK_R)
        qr = _rotate(q_ref[:, sl], cos, sin)
        kr = _rotate(k_ref[:, sl], cos, sin) * SCALE_R
        v = v_ref[:, sl]
        gam = gam_ref[h]
        s = s_ref[h]
        qk = jnp.sum(qr * kr, axis=-1, keepdims=True)
        qs = jnp.dot(jnp.broadcast_to(qr, (8, DK_R)).astype(BF16), s.astype(BF16),
                     preferred_element_type=F32)[0:1]
        o = qk * v + qs * gam
        k_col = jnp.broadcast_to(kr, (DK_R, DK_R)).T
        sout_ref[h] = gam * s + k_col * v
        y_ref[:, sl] = _group_norm_gate(o, gain_ref[:, sl], g_ref[:, sl]).astype(y_ref.dtype)


def _decode_ret(q, k, v, g, state, cos_s, sin_s, gam, gain):
    db = q.shape[0]
    row = pl.BlockSpec((None, 1, W_R), lambda b: (b, 0, 0))
    st = pl.BlockSpec((None, H_R, DK_R, DV_R), lambda b: (b, 0, 0, 0))
    return pl.pallas_call(
        _decode_ret_body,
        grid=(db,),
        in_specs=[row, row, row, row, st,
                  pl.BlockSpec((1, DK_R), lambda b: (0, 0)), pl.BlockSpec((1, DK_R), lambda b: (0, 0)),
                  pl.BlockSpec((H_R, 1, DV_R), lambda b: (0, 0, 0)),
                  pl.BlockSpec((1, W_R), lambda b: (0, 0))],
        out_specs=[row, st],
        out_shape=[jax.ShapeDtypeStruct((db, 1, W_R), BF16),
                   jax.ShapeDtypeStruct((db, H_R, DK_R, DV_R), F32)],
        compiler_params=_cparams(("parallel",), 16),
        name="decode_ret",
    )(q, k, v, g, state, cos_s, sin_s, gam, gain)


def _row_tile(m, pref):
    t = min(m, pref)
    assert m % t == 0
    return t


def kernel(x_prompt, x_sample, cache_k, cache_v, state_ret, page_table, meta_tokens, norm_mix, w_in,
           lambda_q1, lambda_k1, lambda_q2, lambda_k2, attn_subln, ret_subln, w_branch, w_out,
           norm_mlp, w_ff1, w_ff2, norm_final):
    b, seq, d = x_prompt.shape
    db, t_s, _ = x_sample.shape
    assert t_s == 1 and w_in.shape[0] == 1
    n_pages = page_table.shape[1]
    past_len = n_pages * PAGE

    w_in_bf = w_in[0].astype(BF16)
    wa_bf = w_branch[0, 0].astype(BF16)
    wb_bf = w_branch[0, 1].astype(BF16)
    wo_bf = w_out[0].astype(BF16)
    w1_bf = w_ff1[0].astype(BF16)
    w2_bf = w_ff2[0].astype(BF16)
    g_mix = norm_mix[0][None, :]
    g_mlp = norm_mlp[0][None, :]
    g_fin = norm_final[None, :]
    g_attn = attn_subln[0][None, :]
    g_ret = ret_subln[0][None, :]
    lams = (lambda_q1, lambda_k1, lambda_q2, lambda_k2)

    xp = x_prompt.reshape(b * seq, d)
    z = _in_project(xp, g_mix, w_in_bf, _row_tile(b * seq, 1024))
    small_rows = -(-(db + N_META) // 16) * 16
    xs = jnp.concatenate([x_sample.reshape(db, d), meta_tokens.astype(F32),
                          jnp.zeros((small_rows - db - N_META, d), F32)], axis=0)
    zs = _in_project(xs, g_mix, w_in_bf, small_rows)
    z3 = z.reshape(b, seq, -1)

    def cols(arr, c0, n=1):
        return arr[:, c0 * COL:(c0 + n) * COL]

    z_meta = zs[db:db + N_META]
    k_meta, v_meta = cols(z_meta, C_KA), cols(z_meta, C_VA)
    back_pad = ((0, CHUNK - N_META), (0, 0))
    front_pad = ((CHUNK - N_META, 0), (0, 0))

    y_a = _flash_diff(z3, jnp.pad(k_meta, back_pad), jnp.pad(v_meta, back_pad), lams, g_attn,
                      _row_tile(seq, 512))

    n_chunks = seq // CHUNK + 1
    tabs = _retention_tables(n_chunks)
    y_r, s_prompt = _retention(z3, jnp.pad(cols(z_meta, C_KR), front_pad),
                               jnp.pad(cols(z_meta, C_VR), front_pad), tabs, g_ret)

    h1, mn = _branch_out(y_a.reshape(b * seq, W_A), y_r.reshape(b * seq, W_R), z, xp,
                         wa_bf, wb_bf, wo_bf, g_mlp, _row_tile(b * seq, 256))
    y_prompt = _ffn(mn, h1, w1_bf, w2_bf, g_fin, _row_tile(b * seq, 512), 1024).reshape(b, seq, d)

    z_s = zs[:db]
    k_s, v_s = cols(z_s, C_KA), cols(z_s, C_VA)

    def pair_rows(a):
        t = a.reshape(db, H_A, 2, DH_A).transpose(0, 2, 1, 3)
        return jnp.concatenate([t, t], axis=2)

    ya_s = _decode_attn(page_table, pair_rows(cols(z_s, C_QA) * SCALE_A), pair_rows(k_s), pair_rows(v_s),
                        cache_k, cache_v, lams, g_attn.reshape(H_A, HW_A))
    ya_s = ya_s.reshape(db, W_A).astype(BF16)

    half = DK_R // 2
    inv = 1.0 / (ROPE_BASE ** jnp.linspace(0.0, 1.0, half, dtype=F32))
    ang = jnp.full((1,), past_len, jnp.int32).astype(F32)[:, None] * inv[None, :]
    cos_s = jnp.concatenate([jnp.cos(ang), jnp.cos(ang)], axis=-1)
    sin_s = jnp.concatenate([-jnp.sin(ang), jnp.sin(ang)], axis=-1)
    lg = jnp.log1p(-(2.0 ** (-5.0 - jnp.arange(H_R, dtype=F32))))
    gam = jnp.broadcast_to(jnp.exp(lg)[:, None, None], (H_R, 1, DV_R))
    yr_s, s_sample = _decode_ret(cols(z_s, C_QR)[:, None, :], cols(z_s, C_KR)[:, None, :],
                                 cols(z_s, C_VR)[:, None, :], cols(z_s, C_GR)[:, None, :],
                                 state_ret[0], cos_s, sin_s, gam, g_ret)

    xs_rows = x_sample.reshape(db, d)
    h1_s, mn_s = _branch_out(ya_s.reshape(db, W_A), yr_s.reshape(db, W_R), z_s, xs_rows,
                             wa_bf, wb_bf, wo_bf, g_mlp, db)
    y_sample = _ffn(mn_s, h1_s, w1_bf, w2_bf, g_fin, db, 1024).reshape(db, 1, d)

    def with_meta(main, meta):
        full = jnp.concatenate([jnp.broadcast_to(meta[None], (b, N_META, W_A)),
                                main.reshape(b, seq, W_A)], axis=1)
        return full.reshape(1, b, seq + N_META, H_A, HW_A)

    k_prompt = with_meta(cols(z, C_KA), k_meta)
    v_prompt = with_meta(cols(z, C_VA), v_meta)
    return (y_prompt, y_sample, k_prompt, v_prompt, s_prompt[None],
            k_s.reshape(1, db, 1, H_A, HW_A), v_s.reshape(1, db, 1, H_A, HW_A), s_sample[None])
```

```python
import functools
import math

import jax
import jax.numpy as jnp
from jax import lax
from jax.experimental import pallas as pl
from jax.experimental.pallas import tpu as pltpu

F32 = jnp.float32
BF16 = jnp.bfloat16

EPS = 1e-6
N_META = 16
H_A = 4
DH_A = 128
HW_A = 2 * DH_A
W_A = H_A * HW_A
H_R = 8
DK_R = 128
DV_R = 128
W_R = H_R * DV_R
CHUNK = 128
PAGE = 128
ROPE_BASE = 10000.0
LAM_INIT = 0.8 - 0.6 * math.exp(-0.3 * 0)
SCALE_A = DH_A ** -0.5
SCALE_R = DK_R ** -0.5
COL = 1024
C_QA, C_KA, C_VA, C_QR, C_KR, C_VR, C_GR, C_GA = 0, 1, 2, 3, 4, 5, 6, 7

NT_DIMS = (((1,), (1,)), ((), ()))
MIB = 1024 * 1024


def _cparams(sem, vmem_mib):
    return pltpu.CompilerParams(dimension_semantics=sem, vmem_limit_bytes=vmem_mib * MIB)


def _rms(x, gain):
    ms = jnp.mean(x * x, axis=-1, keepdims=True)
    return (x * lax.rsqrt(ms + EPS)) * gain


def _inproj_body(x_ref, g_ref, w_ref, z_ref, a_scr):
    @pl.when(pl.program_id(1) == 0)
    def _():
        a_scr[...] = _rms(x_ref[...], g_ref[...]).astype(BF16)

    z_ref[...] = jnp.dot(a_scr[...], w_ref[...], preferred_element_type=F32)


def _in_project(x, gain, w_bf, tm):
    m, d = x.shape
    n = w_bf.shape[1]
    return pl.pallas_call(
        _inproj_body,
        grid=(m // tm, n // COL),
        in_specs=[pl.BlockSpec((tm, d), lambda i, j: (i, 0)),
                  pl.BlockSpec((1, d), lambda i, j: (0, 0)),
                  pl.BlockSpec((d, COL), lambda i, j: (0, j))],
        out_specs=pl.BlockSpec((tm, COL), lambda i, j: (i, j)),
        out_shape=jax.ShapeDtypeStruct((m, n), F32),
        scratch_shapes=[pltpu.VMEM((tm, d), BF16)],
        compiler_params=_cparams(("parallel", "arbitrary"), 48),
        name="in_project",
    )(x, gain, w_bf)


def _lambda(lq1, lk1, lq2, lk2):
    return (jnp.exp(jnp.sum(lq1[...] * lk1[...], axis=-1, keepdims=True))
            - jnp.exp(jnp.sum(lq2[...] * lk2[...], axis=-1, keepdims=True)) + LAM_INIT)


def _flash_body(q_ref, k_ref, v_ref, km_ref, vm_ref, lq1, lk1, lq2, lk2, gain_ref, o_ref,
                m_scr, l_scr, acc_scr, *, tq):
    qi = pl.program_id(2)
    q = (q_ref[...] * SCALE_A).astype(BF16)

    def scores(kblk):
        kb = kblk.astype(BF16)
        return (lax.dot_general(q[:, :DH_A], kb[:, :DH_A], NT_DIMS, preferred_element_type=F32),
                lax.dot_general(q[:, DH_A:], kb[:, DH_A:], NT_DIMS, preferred_element_type=F32))

    def lane_fold(p):
        return functools.reduce(jnp.add, [p[:, i:i + DH_A] for i in range(0, p.shape[1], DH_A)])

    def first(i, s, vb):
        m = jnp.max(s, axis=-1, keepdims=True)
        p = jnp.exp(s - m)
        m_scr[i] = m
        l_scr[i] = lane_fold(p)
        acc_scr[i] = jnp.dot(p.astype(BF16), vb, preferred_element_type=F32)

    def update(i, s, vb):
        m_old = m_scr[i]
        m_new = jnp.maximum(m_old, jnp.max(s, axis=-1, keepdims=True))
        alpha = jnp.exp(m_old - m_new)
        p = jnp.exp(s - m_new)
        m_scr[i] = m_new
        l_scr[i] = alpha * l_scr[i] + lane_fold(p)
        acc_scr[i] = alpha * acc_scr[i] + jnp.dot(p.astype(BF16), vb, preferred_element_type=F32)

    s1, s2 = scores(km_ref[...])
    meta_ok = lax.broadcasted_iota(jnp.int32, s1.shape, 1) < N_META
    vmb = vm_ref[...].astype(BF16)
    first(0, jnp.where(meta_ok, s1, -jnp.inf), vmb)
    first(1, jnp.where(meta_ok, s2, -jnp.inf), vmb)

    def kv_block(r0, rows):
        return k_ref[pl.ds(r0, rows), :], v_ref[pl.ds(r0, rows), :].astype(BF16)

    def attend(r0, rows):
        kblk, vb = kv_block(r0, rows)
        t1, t2 = scores(kblk)
        update(0, t1, vb)
        update(1, t2, vb)

    def below_diag(j, carry):
        attend(pl.multiple_of(j * 2 * tq, 2 * tq), 2 * tq)
        return carry

    lax.fori_loop(0, qi // 2, below_diag, 0)

    @pl.when(qi % 2 == 1)
    def _():
        attend(pl.multiple_of((qi - 1) * tq, tq), tq)

    kblk, vb = kv_block(pl.multiple_of(qi * tq, tq), tq)
    t1, t2 = scores(kblk)
    causal = (lax.broadcasted_iota(jnp.int32, t1.shape, 1)
              <= lax.broadcasted_iota(jnp.int32, t1.shape, 0))
    update(0, jnp.where(causal, t1, -jnp.inf), vb)
    update(1, jnp.where(causal, t2, -jnp.inf), vb)

    lam = _lambda(lq1, lk1, lq2, lk2)
    l1 = jnp.sum(l_scr[0], axis=-1, keepdims=True)
    l2 = jnp.sum(l_scr[1], axis=-1, keepdims=True)
    o = acc_scr[0] / l1 - lam * (acc_scr[1] / l2)
    o_ref[...] = (_rms(o, gain_ref[...]) * (1.0 - LAM_INIT)).astype(o_ref.dtype)


def _flash_diff(z3, km_pad, vm_pad, lams, gain, tq):
    b, seq, _ = z3.shape
    body = functools.partial(_flash_body, tq=tq)
    vec = pl.BlockSpec((1, DH_A), lambda bi, h, qi: (0, 0))
    return pl.pallas_call(
        body,
        grid=(b, H_A, seq // tq),
        in_specs=[pl.BlockSpec((None, tq, HW_A), lambda bi, h, qi: (bi, qi, h)),
                  pl.BlockSpec((None, seq, HW_A), lambda bi, h, qi: (bi, 0, C_KA * H_A + h)),
                  pl.BlockSpec((None, seq, HW_A), lambda bi, h, qi: (bi, 0, C_VA * H_A + h)),
                  pl.BlockSpec((CHUNK, HW_A), lambda bi, h, qi: (0, h)),
                  pl.BlockSpec((CHUNK, HW_A), lambda bi, h, qi: (0, h)),
                  vec, vec, vec, vec,
                  pl.BlockSpec((1, HW_A), lambda bi, h, qi: (0, h))],
        out_specs=pl.BlockSpec((None, tq, HW_A), lambda bi, h, qi: (bi, qi, h)),
        out_shape=jax.ShapeDtypeStruct((b, seq, W_A), BF16),
        scratch_shapes=[pltpu.VMEM((2, tq, 1), F32), pltpu.VMEM((2, tq, DH_A), F32),
                        pltpu.VMEM((2, tq, HW_A), F32)],
        compiler_params=_cparams(("parallel", "parallel", "arbitrary"), 40),
        name="flash_diff",
    )(z3, z3, z3, km_pad, vm_pad, *lams, gain)


def _rotate(x, cos, sin_signed):
    return x * cos + pltpu.roll(x, DK_R // 2, 1) * sin_signed


def _group_norm_gate(o, gain, gate):
    mu = jnp.mean(o, axis=-1, keepdims=True)
    d = o - mu
    var = jnp.mean(d * d, axis=-1, keepdims=True)
    return (gate * jax.nn.sigmoid(gate)) * ((d * lax.rsqrt(var + EPS)) * gain)


def _ret_body(q_ref, k_ref, v_ref, g_ref, km_ref, vm_ref, cos_ref, sin_ref, cosm_ref, sinm_ref,
              dmat_ref, dq_ref, dk_ref, gc_ref, gain_ref, y_ref, sout_ref, s_scr):
    c = pl.program_id(1)
    hs = lambda h: slice(h * DK_R, (h + 1) * DK_R)

    @pl.when(c == 0)
    def _():
        cosm, sinm = cosm_ref[...], sinm_ref[...]
        for h in range(H_R):
            kr = _rotate(km_ref[:, hs(h)], cosm, sinm) * SCALE_R
            kd = (kr * dk_ref[:, hs(h)]).T.astype(BF16)
            s_scr[h] = jnp.dot(kd, vm_ref[:, hs(h)].astype(BF16), preferred_element_type=F32)

    cos, sin = cos_ref[...], sin_ref[...]
    for h in range(H_R):
        qr = _rotate(q_ref[:, hs(h)], cos, sin)
        kr = _rotate(k_ref[:, hs(h)], cos, sin) * SCALE_R
        vb = v_ref[:, hs(h)].astype(BF16)
        qb = qr.astype(BF16)
        inner = lax.dot_general(qb, kr.astype(BF16), NT_DIMS, preferred_element_type=F32) * dmat_ref[h]
        s = s_scr[h]
        o = (jnp.dot(inner.astype(BF16), vb, preferred_element_type=F32)
             + jnp.dot(qb, s.astype(BF16), preferred_element_type=F32) * dq_ref[:, hs(h)])
        kd = (kr * dk_ref[:, hs(h)]).T.astype(BF16)
        s_scr[h] = gc_ref[h] * s + jnp.dot(kd, vb, preferred_element_type=F32)
        y_ref[:, hs(h)] = _group_norm_gate(o, gain_ref[:, hs(h)], g_ref[:, hs(h)]).astype(y_ref.dtype)

    @pl.when(c == pl.num_programs(1) - 1)
    def _():
        sout_ref[...] = s_scr[...]


def _retention(z3, km_pad, vm_pad, tabs, gain):
    b, seq, _ = z3.shape
    cos_t, sin_t, dmat, dq, dk, gc = tabs
    row = lambda col: pl.BlockSpec((None, CHUNK, COL), lambda bi, c: (bi, c, col))
    fixed2 = lambda shape: pl.BlockSpec(shape, lambda bi, c: (0, 0))
    fixed3 = lambda shape: pl.BlockSpec(shape, lambda bi, c: (0, 0, 0))
    return pl.pallas_call(
        _ret_body,
        grid=(b, seq // CHUNK),
        in_specs=[row(C_QR), row(C_KR), row(C_VR), row(C_GR),
                  fixed2((CHUNK, COL)), fixed2((CHUNK, COL)),
                  pl.BlockSpec((CHUNK, DK_R), lambda bi, c: (c + 1, 0)),
                  pl.BlockSpec((CHUNK, DK_R), lambda bi, c: (c + 1, 0)),
                  fixed2((CHUNK, DK_R)), fixed2((CHUNK, DK_R)),
                  fixed3((H_R, CHUNK, CHUNK)), fixed2((CHUNK, W_R)), fixed2((CHUNK, W_R)),
                  fixed3((H_R, 1, DV_R)), fixed2((1, W_R))],
        out_specs=[pl.BlockSpec((None, CHUNK, W_R), lambda bi, c: (bi, c, 0)),
                   pl.BlockSpec((None, H_R, DK_R, DV_R), lambda bi, c: (bi, 0, 0, 0))],
        out_shape=[jax.ShapeDtypeStruct((b, seq, W_R), BF16),
                   jax.ShapeDtypeStruct((b, H_R, DK_R, DV_R), F32)],
        scratch_shapes=[pltpu.VMEM((H_R, DK_R, DV_R), F32)],
        compiler_params=_cparams(("parallel", "arbitrary"), 32),
        name="retention",
    )(z3, z3, z3, z3, km_pad, vm_pad, cos_t, sin_t, cos_t, sin_t, dmat, dq, dk, gc, gain)


def _retention_tables(n_chunks):
    half = DK_R // 2
    inv = 1.0 / (ROPE_BASE ** jnp.linspace(0.0, 1.0, half, dtype=F32))
    pos = jnp.arange(n_chunks * CHUNK, dtype=jnp.int32) - (CHUNK - N_META)
    ang = pos.astype(F32)[:, None] * inv[None, :]
    cos_t = jnp.concatenate([jnp.cos(ang), jnp.cos(ang)], axis=-1)
    sin_t = jnp.concatenate([-jnp.sin(ang), jnp.sin(ang)], axis=-1)
    lg = jnp.log1p(-(2.0 ** (-5.0 - jnp.arange(H_R, dtype=F32))))
    idx = jnp.arange(CHUNK, dtype=F32)
    diff = idx[:, None] - idx[None, :]
    dmat = jnp.where(diff[None] >= 0, jnp.exp(jnp.maximum(diff, 0.0)[None] * lg[:, None, None]), 0.0)
    dq = jnp.repeat(jnp.exp((idx + 1.0)[:, None] * lg[None, :]), DV_R, axis=1)
    dk = jnp.repeat(jnp.exp((CHUNK - 1.0 - idx)[:, None] * lg[None, :]), DK_R, axis=1)
    gc = jnp.broadcast_to(jnp.exp(CHUNK * lg)[:, None, None], (H_R, 1, DV_R))
    return cos_t, sin_t, dmat, dq, dk, gc


def _branch_body(ya_ref, yr_ref, ga0_ref, ga1_ref, gb0_ref, gb1_ref, x_ref, wa_ref, wb_ref, wo_ref,
                 gm_ref, h_ref, m_ref):
    br_a = jnp.dot(ya_ref[...], wa_ref[...], preferred_element_type=F32)
    br_b = jnp.dot(yr_ref[...], wb_ref[...], preferred_element_type=F32)
    merged = jnp.concatenate(
        [jax.nn.sigmoid(ga0_ref[...]) * br_a[:, :COL] + jax.nn.sigmoid(gb0_ref[...]) * br_b[:, :COL],
         jax.nn.sigmoid(ga1_ref[...]) * br_a[:, COL:] + jax.nn.sigmoid(gb1_ref[...]) * br_b[:, COL:]],
        axis=1)
    h = x_ref[...] + jnp.dot(merged.astype(BF16), wo_ref[...], preferred_element_type=F32)
    h_ref[...] = h
    m_ref[...] = _rms(h, gm_ref[...]).astype(BF16)


def _branch_out(ya, yr, z, x, wa, wb, wo, gm, tm):
    m, d = x.shape
    assert d == 2 * COL
    gate = lambda blk: pl.BlockSpec((tm, COL), lambda i: (i, C_GA + blk))
    resident = lambda shape: pl.BlockSpec(shape, lambda i: (0, 0), pipeline_mode=pl.Buffered(1))
    return pl.pallas_call(
        _branch_body,
        grid=(m // tm,),
        in_specs=[pl.BlockSpec((tm, W_A), lambda i: (i, 0)),
                  pl.BlockSpec((tm, W_R), lambda i: (i, 0)),
                  gate(0), gate(1), gate(2), gate(3),
                  pl.BlockSpec((tm, d), lambda i: (i, 0)),
                  resident((W_A, d)), resident((W_R, d)), resident((d, d)), resident((1, d))],
        out_specs=[pl.BlockSpec((tm, d), lambda i: (i, 0)), pl.BlockSpec((tm, d), lambda i: (i, 0))],
        out_shape=[jax.ShapeDtypeStruct((m, d), F32), jax.ShapeDtypeStruct((m, d), BF16)],
        compiler_params=_cparams(("parallel",), 56),
        name="branch_out",
    )(ya, yr, z, z, z, z, x, wa, wb, wo, gm)


def _ffn_body(m_ref, h_ref, w1_ref, w2_ref, gf_ref, y_ref, acc_scr):
    f = pl.program_id(1)
    u = jnp.maximum(jnp.dot(m_ref[...], w1_ref[...], preferred_element_type=F32), 0.0)
    part = jnp.dot((u * u).astype(BF16), w2_ref[...], preferred_element_type=F32)

    @pl.when(f == 0)
    def _():
        acc_scr[...] = part

    @pl.when(f > 0)
    def _():
        acc_scr[...] += part

    @pl.when(f == pl.num_programs(1) - 1)
    def _():
        y_ref[...] = _rms(h_ref[...] + acc_scr[...], gf_ref[...])


def _ffn(mn, h, w1, w2, gf, tm, tf):
    m, d = h.shape
    dff = w1.shape[1]
    return pl.pallas_call(
        _ffn_body,
        grid=(m // tm, dff // tf),
        in_specs=[pl.BlockSpec((tm, d), lambda i, f: (i, 0)),
                  pl.BlockSpec((tm, d), lambda i, f: (i, 0)),
                  pl.BlockSpec((d, tf), lambda i, f: (0, f)),
                  pl.BlockSpec((tf, d), lambda i, f: (f, 0)),
                  pl.BlockSpec((1, d), lambda i, f: (0, 0))],
        out_specs=pl.BlockSpec((tm, d), lambda i, f: (i, 0)),
        out_shape=jax.ShapeDtypeStruct((m, d), F32),
        scratch_shapes=[pltpu.VMEM((tm, d), F32)],
        compiler_params=_cparams(("parallel", "arbitrary"), 48),
        name="ffn",
    )(mn, h, w1, w2, gf)


PAIRS = PAGE // 2
NCOL = 2 * H_A


def _swap_parity(x):
    return jnp.concatenate([x[:, H_A:], x[:, :H_A]], axis=1)


def _lane_to_sublane(w):
    wide = jnp.broadcast_to(w[:, None, :], (w.shape[0], NCOL, NCOL))
    diag = (lax.broadcasted_iota(jnp.int32, wide.shape, 1)
            == lax.broadcasted_iota(jnp.int32, wide.shape, 2))
    return jnp.sum(jnp.where(diag, wide, 0.0), axis=-1, keepdims=True)


def _page_rows(ref, j):
    return ref[:, :, j * DH_A:(j + 1) * DH_A].reshape(PAIRS, NCOL, DH_A)


def _decode_start(q_ref, kn_ref, vn_ref, m_scr, l_scr, acc_scr):
    first_cols = (lax.broadcasted_iota(jnp.int32, (1, NCOL), 1) < H_A).astype(F32)
    first_rows = lax.broadcasted_iota(jnp.int32, (NCOL, DH_A), 0) < H_A
    for j in range(2):
        m_scr[j] = jnp.sum((kn_ref[j] * q_ref[j])[None], axis=-1)
        l_scr[j] = first_cols
        for half in range(2):
            acc_scr[j, half] = jnp.where(first_rows, vn_ref[half], 0.0)


def _decode_pages(q_ref, k_refs, v_refs, p_scr, m_scr, l_scr, acc_scr):
    n = len(k_refs)
    alphas = []
    for j in range(2):
        top = None
        for k in range(n):
            blk = jnp.sum(_page_rows(k_refs[k], j) * q_ref[j][None], axis=-1)
            p_scr[j, k] = blk
            top = blk if top is None else jnp.maximum(top, blk)
        m_blk = jnp.max(top, axis=0, keepdims=True)
        m_old = m_scr[j]
        m_new = jnp.maximum(m_old, jnp.maximum(m_blk, _swap_parity(m_blk)))
        alpha = jnp.exp(m_old - m_new)
        l_add = jnp.zeros((PAIRS, NCOL), F32)
        for k in range(n):
            p = jnp.exp(p_scr[j, k] - m_new)
            p_scr[j, k] = p
            l_add = l_add + p
        l_scr[j] = l_scr[j] * alpha + jnp.sum(l_add, axis=0, keepdims=True)
        m_scr[j] = m_new
        alphas.append(_lane_to_sublane(alpha)[0])
    acc = [[acc_scr[j, half] * alphas[j] for half in range(2)] for j in range(2)]
    for k in range(n):
        wbs = [_lane_to_sublane(p_scr[j, k]) for j in range(2)]
        for half in range(2):
            v = _page_rows(v_refs[k], half)
            for j in range(2):
                acc[j][half] = acc[j][half] + jnp.sum(wbs[j] * v, axis=0)
    for j in range(2):
        for half in range(2):
            acc_scr[j, half] = acc[j][half]


def _decode_finish(lam, gain_ref, l_scr, acc_scr):
    halves = []
    for half in range(2):
        tot = jnp.zeros((NCOL, DH_A), F32)
        for j in range(2):
            l = l_scr[j] + _swap_parity(l_scr[j])
            coef = _lane_to_sublane((1.0 if j == 0 else -lam) / l)[0]
            tot = tot + acc_scr[j, half] * coef
        halves.append(tot[:H_A] + tot[H_A:])
    o = jnp.concatenate(halves, axis=1)
    return _rms(o, gain_ref[...]) * (1.0 - LAM_INIT)


def _ffn_decode_body(pt_ref, m_ref, h_ref, w1_ref, w2_ref, gf_ref, q_ref, kn_ref, vn_ref,
                     lq1, lk1, lq2, lk2, gain_ref, *rest, g_pages, groups):
    k_refs = rest[:g_pages]
    v_refs = rest[g_pages:2 * g_pages]
    y_ref, ya_ref = rest[2 * g_pages:2 * g_pages + 2]
    acc_scr, p_scr, dm_scr, dl_scr, dacc_scr = rest[2 * g_pages + 2:]
    f = pl.program_id(1)
    n_f = pl.num_programs(1)
    g = (pl.program_id(0) * n_f + f) % groups

    @pl.when(g == 0)
    def _():
        _decode_start(q_ref, kn_ref, vn_ref, dm_scr, dl_scr, dacc_scr)

    _decode_pages(q_ref, k_refs, v_refs, p_scr, dm_scr, dl_scr, dacc_scr)
    u = jnp.maximum(jnp.dot(m_ref[...], w1_ref[...], preferred_element_type=F32), 0.0)
    part = jnp.dot((u * u).astype(BF16), w2_ref[...], preferred_element_type=F32)

    @pl.when(f == 0)
    def _():
        acc_scr[...] = part

    @pl.when(f > 0)
    def _():
        acc_scr[...] += part

    @pl.when(f == n_f - 1)
    def _():
        y_ref[...] = _rms(h_ref[...] + acc_scr[...], gf_ref[...])

    @pl.when(g == groups - 1)
    def _():
        ya_ref[...] = _decode_finish(_lambda(lq1, lk1, lq2, lk2), gain_ref, dl_scr, dacc_scr)


def _ffn_decode(mn, h, w1, w2, gf, tm, tf, page_table, q_t, kn_t, vn_t, cache_k, cache_v, lams, gain_heads):
    m, d = h.shape
    dff = w1.shape[1]
    n_i, n_f = m // tm, dff // tf
    db, n_pages = page_table.shape
    groups = (n_i * n_f) // db
    g_pages = n_pages // groups
    assert groups * db == n_i * n_f and g_pages * groups == n_pages and g_pages >= 1
    body = functools.partial(_ffn_decode_body, g_pages=g_pages, groups=groups)

    def sample(i, f):
        return (i * n_f + f) // groups

    def page_spec(k):
        def index(i, f, pt):
            t = i * n_f + f
            return (0, pt[t // groups, (t % groups) * g_pages + k], 0, 0, 0)
        return pl.BlockSpec((None, None, PAGE, H_A, HW_A), index)

    per_b = lambda shape: pl.BlockSpec(shape, lambda i, f, pt: (sample(i, f),) + (0,) * (len(shape) - 1))
    fixed = lambda shape: pl.BlockSpec(shape, lambda i, f, pt: (0, 0))
    pages = [page_spec(k) for k in range(g_pages)]
    grid_spec = pltpu.PrefetchScalarGridSpec(
        num_scalar_prefetch=1,
        grid=(n_i, n_f),
        in_specs=[pl.BlockSpec((tm, d), lambda i, f, pt: (i, 0)),
                  pl.BlockSpec((tm, d), lambda i, f, pt: (i, 0)),
                  pl.BlockSpec((d, tf), lambda i, f, pt: (0, f)),
                  pl.BlockSpec((tf, d), lambda i, f, pt: (f, 0)),
                  fixed((1, d)),
                  per_b((None, 2, NCOL, DH_A)), per_b((None, 2, NCOL, DH_A)), per_b((None, 2, NCOL, DH_A)),
                  fixed((1, DH_A)), fixed((1, DH_A)), fixed((1, DH_A)), fixed((1, DH_A)),
                  fixed((H_A, HW_A))] + pages + pages,
        out_specs=[pl.BlockSpec((tm, d), lambda i, f, pt: (i, 0)), per_b((None, H_A, HW_A))],
        scratch_shapes=[pltpu.VMEM((tm, d), F32), pltpu.VMEM((2, g_pages, PAIRS, NCOL), F32),
                        pltpu.VMEM((2, 1, NCOL), F32), pltpu.VMEM((2, 1, NCOL), F32),
                        pltpu.VMEM((2, 2, NCOL, DH_A), F32)],
    )
    return pl.pallas_call(
        body,
        grid_spec=grid_spec,
        out_shape=[jax.ShapeDtypeStruct((m, d), F32), jax.ShapeDtypeStruct((db, H_A, HW_A), F32)],
        compiler_params=_cparams(("arbitrary", "arbitrary"), 60),
        name="ffn_decode",
    )(page_table, mn, h, w1, w2, gf, q_t, kn_t, vn_t, *lams, gain_heads,
      *([cache_k] * g_pages), *([cache_v] * g_pages))


def _decode_ret_body(q_ref, k_ref, v_ref, g_ref, s_ref, cos_ref, sin_ref, gam_ref, gain_ref,
                     y_ref, sout_ref):
    cos, sin = cos_ref[...], sin_ref[...]
    for h in range(H_R):
        sl = slice(h * DK_R, (h + 1) * DK_R)
        qr = _rotate(q_ref[:, sl], cos, sin)
        kr = _rotate(k_ref[:, sl], cos, sin) * SCALE_R
        v = v_ref[:, sl]
        gam = gam_ref[h]
        s = s_ref[h]
        qk = jnp.sum(qr * kr, axis=-1, keepdims=True)
        qs = jnp.dot(jnp.broadcast_to(qr, (8, DK_R)).astype(BF16), s.astype(BF16),
                     preferred_element_type=F32)[0:1]
        o = qk * v + qs * gam
        k_col = jnp.broadcast_to(kr, (DK_R, DK_R)).T
        sout_ref[h] = gam * s + k_col * v
        y_ref[:, sl] = _group_norm_gate(o, gain_ref[:, sl], g_ref[:, sl]).astype(y_ref.dtype)


def _decode_ret(q, k, v, g, state, cos_s, sin_s, gam, gain):
    db = q.shape[0]
    row = pl.BlockSpec((None, 1, W_R), lambda b: (b, 0, 0))
    st = pl.BlockSpec((None, H_R, DK_R, DV_R), lambda b: (b, 0, 0, 0))
    return pl.pallas_call(
        _decode_ret_body,
        grid=(db,),
        in_specs=[row, row, row, row, st,
                  pl.BlockSpec((1, DK_R), lambda b: (0, 0)), pl.BlockSpec((1, DK_R), lambda b: (0, 0)),
                  pl.BlockSpec((H_R, 1, DV_R), lambda b: (0, 0, 0)),
                  pl.BlockSpec((1, W_R), lambda b: (0, 0))],
        out_specs=[row, st],
        out_shape=[jax.ShapeDtypeStruct((db, 1, W_R), BF16),
                   jax.ShapeDtypeStruct((db, H_R, DK_R, DV_R), F32)],
        compiler_params=_cparams(("parallel",), 16),
        name="decode_ret",
    )(q, k, v, g, state, cos_s, sin_s, gam, gain)


def _row_tile(m, pref):
    t = min(m, pref)
    assert m % t == 0
    return t


def kernel(x_prompt, x_sample, cache_k, cache_v, state_ret, page_table, meta_tokens, norm_mix, w_in,
           lambda_q1, lambda_k1, lambda_q2, lambda_k2, attn_subln, ret_subln, w_branch, w_out,
           norm_mlp, w_ff1, w_ff2, norm_final):
    b, seq, d = x_prompt.shape
    db, t_s, _ = x_sample.shape
    assert t_s == 1 and w_in.shape[0] == 1
    n_pages = page_table.shape[1]
    past_len = n_pages * PAGE

    w_in_bf = w_in[0].astype(BF16)
    wa_bf = w_branch[0, 0].astype(BF16)
    wb_bf = w_branch[0, 1].astype(BF16)
    wo_bf = w_out[0].astype(BF16)
    w1_bf = w_ff1[0].astype(BF16)
    w2_bf = w_ff2[0].astype(BF16)
    g_mix = norm_mix[0][None, :]
    g_mlp = norm_mlp[0][None, :]
    g_fin = norm_final[None, :]
    g_attn = attn_subln[0][None, :]
    g_ret = ret_subln[0][None, :]
    lams = (lambda_q1, lambda_k1, lambda_q2, lambda_k2)

    xp = x_prompt.reshape(b * seq, d)
    z = _in_project(xp, g_mix, w_in_bf, _row_tile(b * seq, 1024))
    small_rows = -(-(db + N_META) // 16) * 16
    xs = jnp.concatenate([x_sample.reshape(db, d), meta_tokens.astype(F32),
                          jnp.zeros((small_rows - db - N_META, d), F32)], axis=0)
    zs = _in_project(xs, g_mix, w_in_bf, small_rows)
    z3 = z.reshape(b, seq, -1)

    def cols(arr, c0, n=1):
        return arr[:, c0 * COL:(c0 + n) * COL]

    z_meta = zs[db:db + N_META]
    k_meta, v_meta = cols(z_meta, C_KA), cols(z_meta, C_VA)
    back_pad = ((0, CHUNK - N_META), (0, 0))
    front_pad = ((CHUNK - N_META, 0), (0, 0))

    y_a = _flash_diff(z3, jnp.pad(k_meta, back_pad), jnp.pad(v_meta, back_pad), lams, g_attn,
                      _row_tile(seq, 512))

    n_chunks = seq // CHUNK + 1
    tabs = _retention_tables(n_chunks)
    y_r, s_prompt = _retention(z3, jnp.pad(cols(z_meta, C_KR), front_pad),
                               jnp.pad(cols(z_meta, C_VR), front_pad), tabs, g_ret)

    h1, mn = _branch_out(y_a.reshape(b * seq, W_A), y_r.reshape(b * seq, W_R), z, xp,
                         wa_bf, wb_bf, wo_bf, g_mlp, _row_tile(b * seq, 256))

    z_s = zs[:db]
    k_s, v_s = cols(z_s, C_KA), cols(z_s, C_VA)

    def pair_rows(a):
        t = a.reshape(db, H_A, 2, DH_A).transpose(0, 2, 1, 3)
        return jnp.concatenate([t, t], axis=2)

    y_prompt, ya_s = _ffn_decode(mn, h1, w1_bf, w2_bf, g_fin, _row_tile(b * seq, 512), 512, page_table,
                                 pair_rows(cols(z_s, C_QA) * SCALE_A), pair_rows(k_s), pair_rows(v_s),
                                 cache_k, cache_v, lams, g_attn.reshape(H_A, HW_A))
    y_prompt = y_prompt.reshape(b, seq, d)
    ya_s = ya_s.reshape(db, W_A).astype(BF16)

    half = DK_R // 2
    inv = 1.0 / (ROPE_BASE ** jnp.linspace(0.0, 1.0, half, dtype=F32))
    ang = jnp.full((1,), past_len, jnp.int32).astype(F32)[:, None] * inv[None, :]
    cos_s = jnp.concatenate([jnp.cos(ang), jnp.cos(ang)], axis=-1)
    sin_s = jnp.concatenate([-jnp.sin(ang), jnp.sin(ang)], axis=-1)
    lg = jnp.log1p(-(2.0 ** (-5.0 - jnp.arange(H_R, dtype=F32))))
    gam = jnp.broadcast_to(jnp.exp(lg)[:, None, None], (H_R, 1, DV_R))
    yr_s, s_sample = _decode_ret(cols(z_s, C_QR)[:, None, :], cols(z_s, C_KR)[:, None, :],
                                 cols(z_s, C_VR)[:, None, :], cols(z_s, C_GR)[:, None, :],
                                 state_ret[0], cos_s, sin_s, gam, g_ret)

    xs_rows = x_sample.reshape(db, d)
    h1_s, mn_s = _branch_out(ya_s.reshape(db, W_A), yr_s.reshape(db, W_R), z_s, xs_rows,
                             wa_bf, wb_bf, wo_bf, g_mlp, db)
    y_sample = _ffn(mn_s, h1_s, w1_bf, w2_bf, g_fin, db, 1024).reshape(db, 1, d)

    def with_meta(main, meta):
        full = jnp.concatenate([jnp.broadcast_to(meta[None], (b, N_META, W_A)),
                                main.reshape(b, seq, W_A)], axis=1)
        return full.reshape(1, b, seq + N_META, H_A, HW_A)

    k_prompt = with_meta(cols(z, C_KA), k_meta)
    v_prompt = with_meta(cols(z, C_VA), v_meta)
    return (y_prompt, y_sample, k_prompt, v_prompt, s_prompt[None],
            k_s.reshape(1, db, 1, H_A, HW_A), v_s.reshape(1, db, 1, H_A, HW_A), s_sample[None])
```

```python
import functools
import math

import jax
import jax.numpy as jnp
from jax import lax
from jax.experimental import pallas as pl
from jax.experimental.pallas import tpu as pltpu

F32 = jnp.float32
BF16 = jnp.bfloat16

EPS = 1e-6
N_META = 16
H_A = 4
DH_A = 128
HW_A = 2 * DH_A
W_A = H_A * HW_A
H_R = 8
DK_R = 128
DV_R = 128
W_R = H_R * DV_R
CHUNK = 128
PAGE = 128
ROPE_BASE = 10000.0
LAM_INIT = 0.8 - 0.6 * math.exp(-0.3 * 0)
SCALE_A = DH_A ** -0.5
SCALE_R = DK_R ** -0.5
COL = 1024
C_QA, C_KA, C_VA, C_QR, C_KR, C_VR, C_GR, C_GA = 0, 1, 2, 3, 4, 5, 6, 7

NT_DIMS = (((1,), (1,)), ((), ()))
MIB = 1024 * 1024


def _cparams(sem, vmem_mib):
    return pltpu.CompilerParams(dimension_semantics=sem, vmem_limit_bytes=vmem_mib * MIB)


def _rms(x, gain):
    ms = jnp.mean(x * x, axis=-1, keepdims=True)
    return (x * lax.rsqrt(ms + EPS)) * gain


def _inproj_body(x_ref, g_ref, w_ref, z_ref, k_ref, v_ref, a_scr):
    j = pl.program_id(1)

    @pl.when(j == 0)
    def _():
        a_scr[...] = _rms(x_ref[...], g_ref[...]).astype(BF16)

    z = jnp.dot(a_scr[...], w_ref[...], preferred_element_type=F32)
    z_ref[...] = z

    def per_head(o_ref):
        for h in range(H_A):
            o_ref[:, h, :] = z[:, h * HW_A:(h + 1) * HW_A]

    @pl.when(j == C_KA)
    def _():
        per_head(k_ref)

    @pl.when(j == C_VA)
    def _():
        per_head(v_ref)


def _in_project(x, gain, w_bf, tm):
    m, d = x.shape
    n = w_bf.shape[1]
    heads = pl.BlockSpec((tm, H_A, HW_A), lambda i, j: (i, 0, 0))
    return pl.pallas_call(
        _inproj_body,
        grid=(m // tm, n // COL),
        in_specs=[pl.BlockSpec((tm, d), lambda i, j: (i, 0)),
                  pl.BlockSpec((1, d), lambda i, j: (0, 0)),
                  pl.BlockSpec((d, COL), lambda i, j: (0, j))],
        out_specs=[pl.BlockSpec((tm, COL), lambda i, j: (i, j)), heads, heads],
        out_shape=[jax.ShapeDtypeStruct((m, n), F32), jax.ShapeDtypeStruct((m, H_A, HW_A), F32),
                   jax.ShapeDtypeStruct((m, H_A, HW_A), F32)],
        scratch_shapes=[pltpu.VMEM((tm, d), BF16)],
        compiler_params=_cparams(("parallel", "arbitrary"), 60),
        name="in_project",
    )(x, gain, w_bf)


def _lambda(lq1, lk1, lq2, lk2):
    return (jnp.exp(jnp.sum(lq1[...] * lk1[...], axis=-1, keepdims=True))
            - jnp.exp(jnp.sum(lq2[...] * lk2[...], axis=-1, keepdims=True)) + LAM_INIT)


def _flash_body(q_ref, k_ref, v_ref, km_ref, vm_ref, lq1, lk1, lq2, lk2, gain_ref, o_ref,
                m_scr, l_scr, acc_scr, *, tq):
    qi = pl.program_id(2)
    q = (q_ref[...] * SCALE_A).astype(BF16)

    def scores(kblk):
        kb = kblk.astype(BF16)
        return (lax.dot_general(q[:, :DH_A], kb[:, :DH_A], NT_DIMS, preferred_element_type=F32),
                lax.dot_general(q[:, DH_A:], kb[:, DH_A:], NT_DIMS, preferred_element_type=F32))

    def lane_fold(p):
        return functools.reduce(jnp.add, [p[:, i:i + DH_A] for i in range(0, p.shape[1], DH_A)])

    def first(i, s, vb):
        m = jnp.max(s, axis=-1, keepdims=True)
        p = jnp.exp(s - m)
        m_scr[i] = m
        l_scr[i] = lane_fold(p)
        acc_scr[i] = jnp.dot(p.astype(BF16), vb, preferred_element_type=F32)

    def update(i, s, vb):
        m_old = m_scr[i]
        m_new = jnp.maximum(m_old, jnp.max(s, axis=-1, keepdims=True))
        alpha = jnp.exp(m_old - m_new)
        p = jnp.exp(s - m_new)
        m_scr[i] = m_new
        l_scr[i] = alpha * l_scr[i] + lane_fold(p)
        acc_scr[i] = alpha * acc_scr[i] + jnp.dot(p.astype(BF16), vb, preferred_element_type=F32)

    s1, s2 = scores(km_ref[...])
    meta_ok = lax.broadcasted_iota(jnp.int32, s1.shape, 1) < N_META
    vmb = vm_ref[...].astype(BF16)
    first(0, jnp.where(meta_ok, s1, -jnp.inf), vmb)
    first(1, jnp.where(meta_ok, s2, -jnp.inf), vmb)

    def kv_block(r0, rows):
        return k_ref[pl.ds(r0, rows), :], v_ref[pl.ds(r0, rows), :].astype(BF16)

    def attend(r0, rows):
        kblk, vb = kv_block(r0, rows)
        t1, t2 = scores(kblk)
        update(0, t1, vb)
        update(1, t2, vb)

    def below_diag(j, carry):
        attend(pl.multiple_of(j * 2 * tq, 2 * tq), 2 * tq)
        return carry

    lax.fori_loop(0, qi // 2, below_diag, 0)

    @pl.when(qi % 2 == 1)
    def _():
        attend(pl.multiple_of((qi - 1) * tq, tq), tq)

    kblk, vb = kv_block(pl.multiple_of(qi * tq, tq), tq)
    t1, t2 = scores(kblk)
    causal = (lax.broadcasted_iota(jnp.int32, t1.shape, 1)
              <= lax.broadcasted_iota(jnp.int32, t1.shape, 0))
    update(0, jnp.where(causal, t1, -jnp.inf), vb)
    update(1, jnp.where(causal, t2, -jnp.inf), vb)

    lam = _lambda(lq1, lk1, lq2, lk2)
    l1 = jnp.sum(l_scr[0], axis=-1, keepdims=True)
    l2 = jnp.sum(l_scr[1], axis=-1, keepdims=True)
    o = acc_scr[0] / l1 - lam * (acc_scr[1] / l2)
    o_ref[...] = (_rms(o, gain_ref[...]) * (1.0 - LAM_INIT)).astype(o_ref.dtype)


def _flash_diff(z3, km_pad, vm_pad, lams, gain, tq):
    b, seq, _ = z3.shape
    body = functools.partial(_flash_body, tq=tq)
    vec = pl.BlockSpec((1, DH_A), lambda bi, h, qi: (0, 0))
    return pl.pallas_call(
        body,
        grid=(b, H_A, seq // tq),
        in_specs=[pl.BlockSpec((None, tq, HW_A), lambda bi, h, qi: (bi, qi, h)),
                  pl.BlockSpec((None, seq, HW_A), lambda bi, h, qi: (bi, 0, C_KA * H_A + h)),
                  pl.BlockSpec((None, seq, HW_A), lambda bi, h, qi: (bi, 0, C_VA * H_A + h)),
                  pl.BlockSpec((CHUNK, HW_A), lambda bi, h, qi: (0, h)),
                  pl.BlockSpec((CHUNK, HW_A), lambda bi, h, qi: (0, h)),
                  vec, vec, vec, vec,
                  pl.BlockSpec((1, HW_A), lambda bi, h, qi: (0, h))],
        out_specs=pl.BlockSpec((None, tq, HW_A), lambda bi, h, qi: (bi, qi, h)),
        out_shape=jax.ShapeDtypeStruct((b, seq, W_A), BF16),
        scratch_shapes=[pltpu.VMEM((2, tq, 1), F32), pltpu.VMEM((2, tq, DH_A), F32),
                        pltpu.VMEM((2, tq, HW_A), F32)],
        compiler_params=_cparams(("parallel", "parallel", "arbitrary"), 40),
        name="flash_diff",
    )(z3, z3, z3, km_pad, vm_pad, *lams, gain)


def _rotate(x, cos, sin_signed):
    return x * cos + pltpu.roll(x, DK_R // 2, 1) * sin_signed


def _group_norm_gate(o, gain, gate):
    mu = jnp.mean(o, axis=-1, keepdims=True)
    d = o - mu
    var = jnp.mean(d * d, axis=-1, keepdims=True)
    return (gate * jax.nn.sigmoid(gate)) * ((d * lax.rsqrt(var + EPS)) * gain)


def _ret_body(q_ref, k_ref, v_ref, g_ref, km_ref, vm_ref, cos_ref, sin_ref, cosm_ref, sinm_ref,
              dmat_ref, dq_ref, dk_ref, gc_ref, gain_ref, y_ref, sout_ref, s_scr):
    c = pl.program_id(1)
    hs = lambda h: slice(h * DK_R, (h + 1) * DK_R)

    @pl.when(c == 0)
    def _():
        cosm, sinm = cosm_ref[...], sinm_ref[...]
        for h in range(H_R):
            kr = _rotate(km_ref[:, hs(h)], cosm, sinm) * SCALE_R
            kd = (kr * dk_ref[:, hs(h)]).T.astype(BF16)
            s_scr[h] = jnp.dot(kd, vm_ref[:, hs(h)].astype(BF16), preferred_element_type=F32)

    cos, sin = cos_ref[...], sin_ref[...]
    for h in range(H_R):
        qr = _rotate(q_ref[:, hs(h)], cos, sin)
        kr = _rotate(k_ref[:, hs(h)], cos, sin) * SCALE_R
        vb = v_ref[:, hs(h)].astype(BF16)
        qb = qr.astype(BF16)
        inner = lax.dot_general(qb, kr.astype(BF16), NT_DIMS, preferred_element_type=F32) * dmat_ref[h]
        s = s_scr[h]
        o = (jnp.dot(inner.astype(BF16), vb, preferred_element_type=F32)
             + jnp.dot(qb, s.astype(BF16), preferred_element_type=F32) * dq_ref[:, hs(h)])
        kd = (kr * dk_ref[:, hs(h)]).T.astype(BF16)
        s_scr[h] = gc_ref[h] * s + jnp.dot(kd, vb, preferred_element_type=F32)
        y_ref[:, hs(h)] = _group_norm_gate(o, gain_ref[:, hs(h)], g_ref[:, hs(h)]).astype(y_ref.dtype)

    @pl.when(c == pl.num_programs(1) - 1)
    def _():
        sout_ref[...] = s_scr[...]


def _retention(z3, km_pad, vm_pad, tabs, gain):
    b, seq, _ = z3.shape
    cos_t, sin_t, dmat, dq, dk, gc = tabs
    row = lambda col: pl.BlockSpec((None, CHUNK, COL), lambda bi, c: (bi, c, col))
    fixed2 = lambda shape: pl.BlockSpec(shape, lambda bi, c: (0, 0))
    fixed3 = lambda shape: pl.BlockSpec(shape, lambda bi, c: (0, 0, 0))
    return pl.pallas_call(
        _ret_body,
        grid=(b, seq // CHUNK),
        in_specs=[row(C_QR), row(C_KR), row(C_VR), row(C_GR),
                  fixed2((CHUNK, COL)), fixed2((CHUNK, COL)),
                  pl.BlockSpec((CHUNK, DK_R), lambda bi, c: (c + 1, 0)),
                  pl.BlockSpec((CHUNK, DK_R), lambda bi, c: (c + 1, 0)),
                  fixed2((CHUNK, DK_R)), fixed2((CHUNK, DK_R)),
                  fixed3((H_R, CHUNK, CHUNK)), fixed2((CHUNK, W_R)), fixed2((CHUNK, W_R)),
                  fixed3((H_R, 1, DV_R)), fixed2((1, W_R))],
        out_specs=[pl.BlockSpec((None, CHUNK, W_R), lambda bi, c: (bi, c, 0)),
                   pl.BlockSpec((None, H_R, DK_R, DV_R), lambda bi, c: (bi, 0, 0, 0))],
        out_shape=[jax.ShapeDtypeStruct((b, seq, W_R), BF16),
                   jax.ShapeDtypeStruct((b, H_R, DK_R, DV_R), F32)],
        scratch_shapes=[pltpu.VMEM((H_R, DK_R, DV_R), F32)],
        compiler_params=_cparams(("parallel", "arbitrary"), 32),
        name="retention",
    )(z3, z3, z3, z3, km_pad, vm_pad, cos_t, sin_t, cos_t, sin_t, dmat, dq, dk, gc, gain)


def _retention_tables(n_chunks):
    half = DK_R // 2
    inv = 1.0 / (ROPE_BASE ** jnp.linspace(0.0, 1.0, half, dtype=F32))
    pos = jnp.arange(n_chunks * CHUNK, dtype=jnp.int32) - (CHUNK - N_META)
    ang = pos.astype(F32)[:, None] * inv[None, :]
    cos_t = jnp.concatenate([jnp.cos(ang), jnp.cos(ang)], axis=-1)
    sin_t = jnp.concatenate([-jnp.sin(ang), jnp.sin(ang)], axis=-1)
    lg = jnp.log1p(-(2.0 ** (-5.0 - jnp.arange(H_R, dtype=F32))))
    idx = jnp.arange(CHUNK, dtype=F32)
    diff = idx[:, None] - idx[None, :]
    dmat = jnp.where(diff[None] >= 0, jnp.exp(jnp.maximum(diff, 0.0)[None] * lg[:, None, None]), 0.0)
    dq = jnp.repeat(jnp.exp((idx + 1.0)[:, None] * lg[None, :]), DV_R, axis=1)
    dk = jnp.repeat(jnp.exp((CHUNK - 1.0 - idx)[:, None] * lg[None, :]), DK_R, axis=1)
    gc = jnp.broadcast_to(jnp.exp(CHUNK * lg)[:, None, None], (H_R, 1, DV_R))
    return cos_t, sin_t, dmat, dq, dk, gc


def _branch_body(ya_ref, yr_ref, ga0_ref, ga1_ref, gb0_ref, gb1_ref, x_ref, wa_ref, wb_ref, wo_ref,
                 gm_ref, h_ref, m_ref):
    br_a = jnp.dot(ya_ref[...], wa_ref[...], preferred_element_type=F32)
    br_b = jnp.dot(yr_ref[...], wb_ref[...], preferred_element_type=F32)
    merged = jnp.concatenate(
        [jax.nn.sigmoid(ga0_ref[...]) * br_a[:, :COL] + jax.nn.sigmoid(gb0_ref[...]) * br_b[:, :COL],
         jax.nn.sigmoid(ga1_ref[...]) * br_a[:, COL:] + jax.nn.sigmoid(gb1_ref[...]) * br_b[:, COL:]],
        axis=1)
    h = x_ref[...] + jnp.dot(merged.astype(BF16), wo_ref[...], preferred_element_type=F32)
    h_ref[...] = h
    m_ref[...] = _rms(h, gm_ref[...]).astype(BF16)


def _branch_out(ya, yr, z, x, wa, wb, wo, gm, tm):
    m, d = x.shape
    assert d == 2 * COL
    gate = lambda blk: pl.BlockSpec((tm, COL), lambda i: (i, C_GA + blk))
    resident = lambda shape: pl.BlockSpec(shape, lambda i: (0, 0), pipeline_mode=pl.Buffered(1))
    return pl.pallas_call(
        _branch_body,
        grid=(m // tm,),
        in_specs=[pl.BlockSpec((tm, W_A), lambda i: (i, 0)),
                  pl.BlockSpec((tm, W_R), lambda i: (i, 0)),
                  gate(0), gate(1), gate(2), gate(3),
                  pl.BlockSpec((tm, d), lambda i: (i, 0)),
                  resident((W_A, d)), resident((W_R, d)), resident((d, d)), resident((1, d))],
        out_specs=[pl.BlockSpec((tm, d), lambda i: (i, 0)), pl.BlockSpec((tm, d), lambda i: (i, 0))],
        out_shape=[jax.ShapeDtypeStruct((m, d), F32), jax.ShapeDtypeStruct((m, d), BF16)],
        compiler_params=_cparams(("parallel",), 56),
        name="branch_out",
    )(ya, yr, z, z, z, z, x, wa, wb, wo, gm)


def _ffn_body(m_ref, h_ref, w1_ref, w2_ref, gf_ref, y_ref, acc_scr):
    f = pl.program_id(1)
    u = jnp.maximum(jnp.dot(m_ref[...], w1_ref[...], preferred_element_type=F32), 0.0)
    part = jnp.dot((u * u).astype(BF16), w2_ref[...], preferred_element_type=F32)

    @pl.when(f == 0)
    def _():
        acc_scr[...] = part

    @pl.when(f > 0)
    def _():
        acc_scr[...] += part

    @pl.when(f == pl.num_programs(1) - 1)
    def _():
        y_ref[...] = _rms(h_ref[...] + acc_scr[...], gf_ref[...])


def _ffn(mn, h, w1, w2, gf, tm, tf):
    m, d = h.shape
    dff = w1.shape[1]
    return pl.pallas_call(
        _ffn_body,
        grid=(m // tm, dff // tf),
        in_specs=[pl.BlockSpec((tm, d), lambda i, f: (i, 0)),
                  pl.BlockSpec((tm, d), lambda i, f: (i, 0)),
                  pl.BlockSpec((d, tf), lambda i, f: (0, f)),
                  pl.BlockSpec((tf, d), lambda i, f: (f, 0)),
                  pl.BlockSpec((1, d), lambda i, f: (0, 0))],
        out_specs=pl.BlockSpec((tm, d), lambda i, f: (i, 0)),
        out_shape=jax.ShapeDtypeStruct((m, d), F32),
        scratch_shapes=[pltpu.VMEM((tm, d), F32)],
        compiler_params=_cparams(("parallel", "arbitrary"), 48),
        name="ffn",
    )(mn, h, w1, w2, gf)


PAIRS = PAGE // 2
NCOL = 2 * H_A


def _swap_parity(x):
    return jnp.concatenate([x[:, H_A:], x[:, :H_A]], axis=1)


def _lane_to_sublane(w):
    wide = jnp.broadcast_to(w[:, None, :], (w.shape[0], NCOL, NCOL))
    diag = (lax.broadcasted_iota(jnp.int32, wide.shape, 1)
            == lax.broadcasted_iota(jnp.int32, wide.shape, 2))
    return jnp.sum(jnp.where(diag, wide, 0.0), axis=-1, keepdims=True)


def _page_rows(ref, j):
    return ref[:, :, j * DH_A:(j + 1) * DH_A].reshape(PAIRS, NCOL, DH_A)


def _decode_start(q_ref, kn_ref, vn_ref, m_scr, l_scr, acc_scr):
    first_cols = (lax.broadcasted_iota(jnp.int32, (1, NCOL), 1) < H_A).astype(F32)
    first_rows = lax.broadcasted_iota(jnp.int32, (NCOL, DH_A), 0) < H_A
    for j in range(2):
        m_scr[j] = jnp.sum((kn_ref[j] * q_ref[j])[None], axis=-1)
        l_scr[j] = first_cols
        for half in range(2):
            acc_scr[j, half] = jnp.where(first_rows, vn_ref[half], 0.0)


def _decode_pages(q_ref, k_refs, v_refs, p_scr, m_scr, l_scr, acc_scr):
    n = len(k_refs)
    alphas = []
    for j in range(2):
        top = None
        for k in range(n):
            blk = jnp.sum(_page_rows(k_refs[k], j) * q_ref[j][None], axis=-1)
            p_scr[j, k] = blk
            top = blk if top is None else jnp.maximum(top, blk)
        m_blk = jnp.max(top, axis=0, keepdims=True)
        m_old = m_scr[j]
        m_new = jnp.maximum(m_old, jnp.maximum(m_blk, _swap_parity(m_blk)))
        alpha = jnp.exp(m_old - m_new)
        l_add = jnp.zeros((PAIRS, NCOL), F32)
        for k in range(n):
            p = jnp.exp(p_scr[j, k] - m_new)
            p_scr[j, k] = p
            l_add = l_add + p
        l_scr[j] = l_scr[j] * alpha + jnp.sum(l_add, axis=0, keepdims=True)
        m_scr[j] = m_new
        alphas.append(_lane_to_sublane(alpha)[0])
    acc = [[acc_scr[j, half] * alphas[j] for half in range(2)] for j in range(2)]
    for k in range(n):
        wbs = [_lane_to_sublane(p_scr[j, k]) for j in range(2)]
        for half in range(2):
            v = _page_rows(v_refs[k], half)
            for j in range(2):
                acc[j][half] = acc[j][half] + jnp.sum(wbs[j] * v, axis=0)
    for j in range(2):
        for half in range(2):
            acc_scr[j, half] = acc[j][half]


def _decode_finish(lam, gain_ref, l_scr, acc_scr):
    halves = []
    for half in range(2):
        tot = jnp.zeros((NCOL, DH_A), F32)
        for j in range(2):
            l = l_scr[j] + _swap_parity(l_scr[j])
            coef = _lane_to_sublane((1.0 if j == 0 else -lam) / l)[0]
            tot = tot + acc_scr[j, half] * coef
        halves.append(tot[:H_A] + tot[H_A:])
    o = jnp.concatenate(halves, axis=1)
    return _rms(o, gain_ref[...]) * (1.0 - LAM_INIT)


def _ffn_decode_body(pt_ref, m_ref, h_ref, w1_ref, w2_ref, gf_ref, q_ref, kn_ref, vn_ref,
                     lq1, lk1, lq2, lk2, gain_ref, *rest, g_pages, groups):
    k_refs = rest[:g_pages]
    v_refs = rest[g_pages:2 * g_pages]
    y_ref, ya_ref = rest[2 * g_pages:2 * g_pages + 2]
    acc_scr, p_scr, dm_scr, dl_scr, dacc_scr = rest[2 * g_pages + 2:]
    f = pl.program_id(1)
    n_f = pl.num_programs(1)
    g = (pl.program_id(0) * n_f + f) % groups

    @pl.when(g == 0)
    def _():
        _decode_start(q_ref, kn_ref, vn_ref, dm_scr, dl_scr, dacc_scr)

    _decode_pages(q_ref, k_refs, v_refs, p_scr, dm_scr, dl_scr, dacc_scr)
    u = jnp.maximum(jnp.dot(m_ref[...], w1_ref[...], preferred_element_type=F32), 0.0)
    part = jnp.dot((u * u).astype(BF16), w2_ref[...], preferred_element_type=F32)

    @pl.when(f == 0)
    def _():
        acc_scr[...] = part

    @pl.when(f > 0)
    def _():
        acc_scr[...] += part

    @pl.when(f == n_f - 1)
    def _():
        y_ref[...] = _rms(h_ref[...] + acc_scr[...], gf_ref[...])

    @pl.when(g == groups - 1)
    def _():
        ya_ref[...] = _decode_finish(_lambda(lq1, lk1, lq2, lk2), gain_ref, dl_scr, dacc_scr)


def _ffn_decode(mn, h, w1, w2, gf, tm, tf, page_table, q_t, kn_t, vn_t, cache_k, cache_v, lams, gain_heads):
    m, d = h.shape
    dff = w1.shape[1]
    n_i, n_f = m // tm, dff // tf
    db, n_pages = page_table.shape
    groups = (n_i * n_f) // db
    g_pages = n_pages // groups
    assert groups * db == n_i * n_f and g_pages * groups == n_pages and g_pages >= 1
    body = functools.partial(_ffn_decode_body, g_pages=g_pages, groups=groups)

    def sample(i, f):
        return (i * n_f + f) // groups

    def page_spec(k):
        def index(i, f, pt):
            t = i * n_f + f
            return (0, pt[t // groups, (t % groups) * g_pages + k], 0, 0, 0)
        return pl.BlockSpec((None, None, PAGE, H_A, HW_A), index)

    per_b = lambda shape: pl.BlockSpec(shape, lambda i, f, pt: (sample(i, f),) + (0,) * (len(shape) - 1))
    fixed = lambda shape: pl.BlockSpec(shape, lambda i, f, pt: (0, 0))
    pages = [page_spec(k) for k in range(g_pages)]
    grid_spec = pltpu.PrefetchScalarGridSpec(
        num_scalar_prefetch=1,
        grid=(n_i, n_f),
        in_specs=[pl.BlockSpec((tm, d), lambda i, f, pt: (i, 0)),
                  pl.BlockSpec((tm, d), lambda i, f, pt: (i, 0)),
                  pl.BlockSpec((d, tf), lambda i, f, pt: (0, f)),
                  pl.BlockSpec((tf, d), lambda i, f, pt: (f, 0)),
                  fixed((1, d)),
                  per_b((None, 2, NCOL, DH_A)), per_b((None, 2, NCOL, DH_A)), per_b((None, 2, NCOL, DH_A)),
                  fixed((1, DH_A)), fixed((1, DH_A)), fixed((1, DH_A)), fixed((1, DH_A)),
                  fixed((H_A, HW_A))] + pages + pages,
        out_specs=[pl.BlockSpec((tm, d), lambda i, f, pt: (i, 0)), per_b((None, H_A, HW_A))],
        scratch_shapes=[pltpu.VMEM((tm, d), F32), pltpu.VMEM((2, g_pages, PAIRS, NCOL), F32),
                        pltpu.VMEM((2, 1, NCOL), F32), pltpu.VMEM((2, 1, NCOL), F32),
                        pltpu.VMEM((2, 2, NCOL, DH_A), F32)],
    )
    return pl.pallas_call(
        body,
        grid_spec=grid_spec,
        out_shape=[jax.ShapeDtypeStruct((m, d), F32), jax.ShapeDtypeStruct((db, H_A, HW_A), F32)],
        compiler_params=_cparams(("arbitrary", "arbitrary"), 60),
        name="ffn_decode",
    )(page_table, mn, h, w1, w2, gf, q_t, kn_t, vn_t, *lams, gain_heads,
      *([cache_k] * g_pages), *([cache_v] * g_pages))


def _decode_ret_body(q_ref, k_ref, v_ref, g_ref, s_ref, cos_ref, sin_ref, gam_ref, gain_ref,
                     y_ref, sout_ref):
    cos, sin = cos_ref[...], sin_ref[...]
    for h in range(H_R):
        sl = slice(h * DK_R, (h + 1) * DK_R)
        qr = _rotate(q_ref[:, sl], cos, sin)
        kr = _rotate(k_ref[:, sl], cos, sin) * SCALE_R
        v = v_ref[:, sl]
        gam = gam_ref[h]
        s = s_ref[h]
        qk = jnp.sum(qr * kr, axis=-1, keepdims=True)
        qs = jnp.dot(jnp.broadcast_to(qr, (8, DK_R)).astype(BF16), s.astype(BF16),
                     preferred_element_type=F32)[0:1]
        o = qk * v + qs * gam
        k_col = jnp.broadcast_to(kr, (DK_R, DK_R)).T
        sout_ref[h] = gam * s + k_col * v
        y_ref[:, sl] = _group_norm_gate(o, gain_ref[:, sl], g_ref[:, sl]).astype(y_ref.dtype)


def _decode_ret(q, k, v, g, state, cos_s, sin_s, gam, gain):
    db = q.shape[0]
    row = pl.BlockSpec((None, 1, W_R), lambda b: (b, 0, 0))
    st = pl.BlockSpec((None, H_R, DK_R, DV_R), lambda b: (b, 0, 0, 0))
    return pl.pallas_call(
        _decode_ret_body,
        grid=(db,),
        in_specs=[row, row, row, row, st,
                  pl.BlockSpec((1, DK_R), lambda b: (0, 0)), pl.BlockSpec((1, DK_R), lambda b: (0, 0)),
                  pl.BlockSpec((H_R, 1, DV_R), lambda b: (0, 0, 0)),
                  pl.BlockSpec((1, W_R), lambda b: (0, 0))],
        out_specs=[row, st],
        out_shape=[jax.ShapeDtypeStruct((db, 1, W_R), BF16),
                   jax.ShapeDtypeStruct((db, H_R, DK_R, DV_R), F32)],
        compiler_params=_cparams(("parallel",), 16),
        name="decode_ret",
    )(q, k, v, g, state, cos_s, sin_s, gam, gain)


def _row_tile(m, pref):
    t = min(m, pref)
    assert m % t == 0
    return t


def kernel(x_prompt, x_sample, cache_k, cache_v, state_ret, page_table, meta_tokens, norm_mix, w_in,
           lambda_q1, lambda_k1, lambda_q2, lambda_k2, attn_subln, ret_subln, w_branch, w_out,
           norm_mlp, w_ff1, w_ff2, norm_final):
    b, seq, d = x_prompt.shape
    db, t_s, _ = x_sample.shape
    assert t_s == 1 and w_in.shape[0] == 1
    n_pages = page_table.shape[1]
    past_len = n_pages * PAGE

    w_in_bf = w_in[0].astype(BF16)
    wa_bf = w_branch[0, 0].astype(BF16)
    wb_bf = w_branch[0, 1].astype(BF16)
    wo_bf = w_out[0].astype(BF16)
    w1_bf = w_ff1[0].astype(BF16)
    w2_bf = w_ff2[0].astype(BF16)
    g_mix = norm_mix[0][None, :]
    g_mlp = norm_mlp[0][None, :]
    g_fin = norm_final[None, :]
    g_attn = attn_subln[0][None, :]
    g_ret = ret_subln[0][None, :]
    lams = (lambda_q1, lambda_k1, lambda_q2, lambda_k2)

    xp = x_prompt.reshape(b * seq, d)
    z, k_heads, v_heads = _in_project(xp, g_mix, w_in_bf, _row_tile(b * seq, 1024))
    small_rows = -(-(db + N_META) // 16) * 16
    xs = jnp.concatenate([x_sample.reshape(db, d), meta_tokens.astype(F32),
                          jnp.zeros((small_rows - db - N_META, d), F32)], axis=0)
    zs, _, _ = _in_project(xs, g_mix, w_in_bf, small_rows)
    z3 = z.reshape(b, seq, -1)

    def cols(arr, c0, n=1):
        return arr[:, c0 * COL:(c0 + n) * COL]

    z_meta = zs[db:db + N_META]
    k_meta, v_meta = cols(z_meta, C_KA), cols(z_meta, C_VA)
    back_pad = ((0, CHUNK - N_META), (0, 0))
    front_pad = ((CHUNK - N_META, 0), (0, 0))

    y_a = _flash_diff(z3, jnp.pad(k_meta, back_pad), jnp.pad(v_meta, back_pad), lams, g_attn,
                      _row_tile(seq, 512))

    n_chunks = seq // CHUNK + 1
    tabs = _retention_tables(n_chunks)
    y_r, s_prompt = _retention(z3, jnp.pad(cols(z_meta, C_KR), front_pad),
                               jnp.pad(cols(z_meta, C_VR), front_pad), tabs, g_ret)

    h1, mn = _branch_out(y_a.reshape(b * seq, W_A), y_r.reshape(b * seq, W_R), z, xp,
                         wa_bf, wb_bf, wo_bf, g_mlp, _row_tile(b * seq, 256))

    z_s = zs[:db]
    k_s, v_s = cols(z_s, C_KA), cols(z_s, C_VA)

    def pair_rows(a):
        t = a.reshape(db, H_A, 2, DH_A).transpose(0, 2, 1, 3)
        return jnp.concatenate([t, t], axis=2)

    y_prompt, ya_s = _ffn_decode(mn, h1, w1_bf, w2_bf, g_fin, _row_tile(b * seq, 512), 512, page_table,
                                 pair_rows(cols(z_s, C_QA) * SCALE_A), pair_rows(k_s), pair_rows(v_s),
                                 cache_k, cache_v, lams, g_attn.reshape(H_A, HW_A))
    y_prompt = y_prompt.reshape(b, seq, d)
    ya_s = ya_s.reshape(db, W_A).astype(BF16)

    half = DK_R // 2
    inv = 1.0 / (ROPE_BASE ** jnp.linspace(0.0, 1.0, half, dtype=F32))
    ang = jnp.full((1,), past_len, jnp.int32).astype(F32)[:, None] * inv[None, :]
    cos_s = jnp.concatenate([jnp.cos(ang), jnp.cos(ang)], axis=-1)
    sin_s = jnp.concatenate([-jnp.sin(ang), jnp.sin(ang)], axis=-1)
    lg = jnp.log1p(-(2.0 ** (-5.0 - jnp.arange(H_R, dtype=F32))))
    gam = jnp.broadcast_to(jnp.exp(lg)[:, None, None], (H_R, 1, DV_R))
    yr_s, s_sample = _decode_ret(cols(z_s, C_QR)[:, None, :], cols(z_s, C_KR)[:, None, :],
                                 cols(z_s, C_VR)[:, None, :], cols(z_s, C_GR)[:, None, :],
                                 state_ret[0], cos_s, sin_s, gam, g_ret)

    xs_rows = x_sample.reshape(db, d)
    h1_s, mn_s = _branch_out(ya_s.reshape(db, W_A), yr_s.reshape(db, W_R), z_s, xs_rows,
                             wa_bf, wb_bf, wo_bf, g_mlp, db)
    y_sample = _ffn(mn_s, h1_s, w1_bf, w2_bf, g_fin, db, 1024).reshape(db, 1, d)

    def with_meta(main, meta):
        meta_rows = jnp.broadcast_to(meta.reshape(1, N_META, H_A, HW_A), (b, N_META, H_A, HW_A))
        return jnp.concatenate([meta_rows, main.reshape(b, seq, H_A, HW_A)], axis=1)[None]

    k_prompt = with_meta(k_heads, k_meta)
    v_prompt = with_meta(v_heads, v_meta)
    return (y_prompt, y_sample, k_prompt, v_prompt, s_prompt[None],
            k_s.reshape(1, db, 1, H_A, HW_A), v_s.reshape(1, db, 1, H_A, HW_A), s_sample[None])
```
